```python
import jax, jax.numpy as jnp
from jax import lax
import numpy as np

D_MODEL = 2048
BATCH = 4
SEQ = 8192
DEPTH = 1

NORM_EPS = 1e-6
RWKV_HEAD_DIM = 64
RWKV_WIDTH = D_MODEL // 2
RWKV_HEADS = RWKV_WIDTH // RWKV_HEAD_DIM
DECAY_LORA = max(32, int(round(D_MODEL ** 0.5 * 1.8 / 32)) * 32)
AAA_LORA = max(32, int(round(D_MODEL ** 0.5 * 1.8 / 32)) * 32)
GATE_LORA = max(32, int(round(D_MODEL ** 0.8 * 0.6 / 32)) * 32)
GN_EPS = 64e-5
MOBA_HEAD_DIM = 128
MOBA_WIDTH = D_MODEL // 2
MOBA_HEADS = MOBA_WIDTH // MOBA_HEAD_DIM
MOBA_BLOCK = 256
MOBA_TOPK = 3
MOBA_Q_CHUNK = 32
NEG_INF = -1e30
D_FF = 4 * D_MODEL
PLE_DIM = 256
RWKV_COLS = 3 * RWKV_WIDTH + DECAY_LORA + AAA_LORA + GATE_LORA
MOBA_COLS = 3 * MOBA_WIDTH
GATE_COLS = 2 * D_MODEL
IN_COLS = RWKV_COLS + MOBA_COLS + GATE_COLS

kernel_name = "hybrid_rwkv7_moba_gated_block"


def rms_norm(x, g):
    xf = x.astype(jnp.float32)
    y = xf * lax.rsqrt(jnp.mean(xf * xf, axis=-1, keepdims=True) + NORM_EPS)
    return (y * g.astype(jnp.float32)).astype(x.dtype)


def alibi_slopes(n_heads):
    return 2.0 ** (-8.0 * (jnp.arange(n_heads, dtype=jnp.float32) + 1.0) / n_heads)


def wkv7_scan(r, w, k, v, kk, a):
    B, S, H, N = r.shape

    def step(state, inp):
        r_t, w_t, k_t, v_t, kk_t, a_t = inp
        sa = jnp.einsum('bhvk,bhk->bhv', state, -kk_t)
        state = (state * w_t[:, :, None, :]
                 + sa[..., None] * (kk_t * a_t)[:, :, None, :]
                 + v_t[..., None] * k_t[:, :, None, :])
        y_t = jnp.einsum('bhvk,bhk->bhv', state, r_t)
        return state, y_t

    xs = tuple(jnp.moveaxis(t, 1, 0) for t in (r, w, k, v, kk, a))
    state0 = jnp.zeros((B, H, N, N), jnp.float32)
    _, y = lax.scan(step, state0, xs)
    return jnp.moveaxis(y, 0, 1)


def rwkv7_time_mix(z, mu, w0, w_decay_up, a0, w_aaa_up, w_gate_up, k_k, k_a, r_k, gn_w, gn_b):
    B, S, _ = z.shape
    H, N, W = RWKV_HEADS, RWKV_HEAD_DIM, RWKV_WIDTH
    f32 = jnp.float32
    z_prev = jnp.pad(z, ((0, 0), (1, 0), (0, 0)))[:, :S]
    z = z + mu * (z_prev - z)
    r, k, v, wd, ad, gd = jnp.split(
        z, [W, 2 * W, 3 * W, 3 * W + DECAY_LORA, 3 * W + DECAY_LORA + AAA_LORA], axis=-1)
    w_log = -jax.nn.softplus(-(w0 + jnp.tanh(wd) @ w_decay_up).astype(f32)) - 0.5
    decay = jnp.exp(-jnp.exp(w_log))
    a = jax.nn.sigmoid(a0 + ad @ w_aaa_up)
    g = jax.nn.sigmoid(gd) @ w_gate_up
    heads = lambda t: t.reshape(B, S, H, N).astype(f32)
    kk = heads(k * k_k)
    kk = kk / jnp.maximum(jnp.sqrt(jnp.sum(kk * kk, axis=-1, keepdims=True)), 1e-12)
    k = k * (1 + (a - 1) * k_a)
    r_h, k_h, v_h, a_h = heads(r), heads(k), heads(v), heads(a)
    y = wkv7_scan(r_h, heads(decay), k_h, v_h, kk, a_h)
    mean = jnp.mean(y, axis=-1, keepdims=True)
    var = jnp.mean(jnp.square(y - mean), axis=-1, keepdims=True)
    y = ((y - mean) * lax.rsqrt(var + GN_EPS)).reshape(B, S, W) * gn_w + gn_b
    bonus = jnp.sum(r_h * k_h * r_k, axis=-1, keepdims=True) * v_h
    out = (y + bonus.reshape(B, S, W)) * g.astype(f32)
    return out.astype(z.dtype)


def moba_attention(q, k, v):
    B, S, H, Dh = q.shape
    BLK, QC = MOBA_BLOCK, MOBA_Q_CHUNK
    nb = -(-S // BLK)
    pad = nb * BLK - S
    f32 = jnp.float32
    q = q.transpose(0, 2, 1, 3)
    kp = jnp.pad(k.transpose(0, 2, 1, 3), ((0, 0), (0, 0), (0, pad), (0, 0)))
    vp = jnp.pad(v.transpose(0, 2, 1, 3), ((0, 0), (0, 0), (0, pad), (0, 0)))
    kb = kp.reshape(B, H, nb, BLK, Dh)
    vb = vp.reshape(B, H, nb, BLK, Dh)
    kmean = jnp.mean(kb.astype(f32), axis=3)
    pos = jnp.arange(S)
    qblk = pos // BLK
    gate = jnp.einsum('bhsd,bhnd->bhsn', q.astype(f32), kmean)
    past = jnp.arange(nb)[None, :] < qblk[:, None]
    gate = jnp.where(past, gate, NEG_INF)
    n_sel = min(MOBA_TOPK, nb)
    _, sel = lax.top_k(gate, n_sel)
    sel_valid = sel < qblk[:, None]

    slopes = alibi_slopes(H)
    scale = Dh ** -0.5
    nc = S // QC
    kb_flat = kb.reshape(B * H * nb, BLK, Dh)
    vb_flat = vb.reshape(B * H * nb, BLK, Dh)
    base = ((jnp.arange(B)[:, None] * H + jnp.arange(H)[None, :]) * nb)[:, :, None, None]
    to_chunks = lambda t: t.reshape(B, H, nc, QC, t.shape[-1]).transpose(2, 0, 1, 3, 4)
    q_c, sel_c, valid_c = to_chunks(q), to_chunks(sel), to_chunks(sel_valid)

    def chunk(args):
        c, qq, ss, vv = args
        t = c * QC + jnp.arange(QC)
        blk = (c * QC) // BLK
        k_own = lax.dynamic_index_in_dim(kb, blk, axis=2, keepdims=False)
        v_own = lax.dynamic_index_in_dim(vb, blk, axis=2, keepdims=False)
        s_own = blk * BLK + jnp.arange(BLK)
        dist_own = t[:, None] - s_own[None, :]
        lo = jnp.einsum('bhqd,bhsd->bhqs', qq, k_own).astype(f32) * scale
        lo = lo - slopes[:, None, None] * dist_own
        lo = jnp.where(dist_own >= 0, lo, NEG_INF)
        k_sel = kb_flat[ss + base]
        v_sel = vb_flat[ss + base]
        s_sel = ss[..., None] * BLK + jnp.arange(BLK)
        dist_sel = t[:, None, None] - s_sel
        ls = jnp.einsum('bhqd,bhqnsd->bhqns', qq, k_sel).astype(f32) * scale
        ls = ls - slopes[:, None, None, None] * dist_sel
        ls = jnp.where(vv[..., None], ls, NEG_INF)
        logits = jnp.concatenate([ls.reshape(B, H, QC, n_sel * BLK), lo], axis=-1)
        probs = jax.nn.softmax(logits, axis=-1)
        p_sel = probs[..., :n_sel * BLK].reshape(B, H, QC, n_sel, BLK).astype(v.dtype)
        p_own = probs[..., n_sel * BLK:].astype(v.dtype)
        return (jnp.einsum('bhqns,bhqnsd->bhqd', p_sel, v_sel)
                + jnp.einsum('bhqs,bhsd->bhqd', p_own, v_own))

    out = lax.map(chunk, (jnp.arange(nc), q_c, sel_c, valid_c))
    return out.transpose(1, 0, 3, 2, 4).reshape(B, S, H, Dh)


def setup_inputs(seed: int = 0) -> dict:
    key = jax.random.key(seed)
    ks = jax.random.split(key, 26)
    L, D, W, Wm = DEPTH, D_MODEL, RWKV_WIDTH, MOBA_WIDTH
    f32 = jnp.float32
    nrm = lambda k, shape, fan_in: jax.random.normal(k, shape, f32) * fan_in ** -0.5
    std = lambda k, shape: jax.random.normal(k, shape, f32)
    return {
        "x": std(ks[0], (BATCH, SEQ, D)),
        "p": std(ks[1], (DEPTH, BATCH, SEQ, PLE_DIM)),
        "g_mix": 1.0 + 0.02 * std(ks[2], (L, D)),
        "w_in": nrm(ks[3], (L, D, IN_COLS), D),
        "mu_shift": jax.random.uniform(ks[4], (L, RWKV_COLS), f32),
        "w0": jax.random.uniform(ks[5], (L, W), f32, minval=-6.5, maxval=-1.5),
        "w_decay_up": 0.1 * nrm(ks[6], (L, DECAY_LORA, W), DECAY_LORA),
        "a0": 0.1 * std(ks[7], (L, W)),
        "w_aaa_up": nrm(ks[8], (L, AAA_LORA, W), AAA_LORA),
        "w_gate_up": nrm(ks[9], (L, GATE_LORA, W), GATE_LORA),
        "k_k": 0.85 + 0.02 * std(ks[10], (L, W)),
        "k_a": 1.0 + 0.02 * std(ks[11], (L, W)),
        "r_k": 0.1 * std(ks[12], (L, RWKV_HEADS, RWKV_HEAD_DIM)),
        "gn_w": 1.0 + 0.02 * std(ks[13], (L, W)),
        "gn_b": 0.01 * std(ks[14], (L, W)),
        "w_branch_rwkv": nrm(ks[15], (L, W, D), W),
        "w_branch_moba": nrm(ks[16], (L, Wm, D), Wm),
        "w_out": nrm(ks[17], (L, D, D), D),
        "g_mlp": 1.0 + 0.02 * std(ks[18], (L, D)),
        "w_mlp_in": nrm(ks[19], (L, D, D_FF), D),
        "w_mlp_out": nrm(ks[20], (L, D_FF, D), D_FF),
        "g_ple": 1.0 + 0.02 * std(ks[21], (L, D)),
        "w_ple_gate": nrm(ks[22], (L, D, D), D),
        "w_ple_up": nrm(ks[23], (L, PLE_DIM, D), PLE_DIM),
        "g_final": 1.0 + 0.02 * std(ks[24], (D,)),
    }


def reference(x, p, g_mix, w_in, mu_shift, w0, w_decay_up, a0, w_aaa_up, w_gate_up,
              k_k, k_a, r_k, gn_w, gn_b, w_branch_rwkv, w_branch_moba, w_out,
              g_mlp, w_mlp_in, w_mlp_out, g_ple, w_ple_gate, w_ple_up, g_final):
    B, S, D = x.shape
    for i in range(DEPTH):
        h = rms_norm(x, g_mix[i])
        z = h @ w_in[i]
        z_rwkv = z[..., :RWKV_COLS]
        z_moba = z[..., RWKV_COLS:RWKV_COLS + MOBA_COLS]
        z_gate = z[..., RWKV_COLS + MOBA_COLS:]
        y_rwkv = rwkv7_time_mix(z_rwkv, mu_shift[i], w0[i], w_decay_up[i], a0[i],
                                w_aaa_up[i], w_gate_up[i], k_k[i], k_a[i], r_k[i],
                                gn_w[i], gn_b[i])
        q, k, v = jnp.split(z_moba, 3, axis=-1)
        to_heads = lambda t: t.reshape(B, S, MOBA_HEADS, MOBA_HEAD_DIM)
        y_moba = moba_attention(to_heads(q), to_heads(k), to_heads(v)).reshape(B, S, MOBA_WIDTH)
        gates = jax.nn.sigmoid(z_gate)
        merged = (gates[..., :D] * (y_rwkv @ w_branch_rwkv[i])
                  + gates[..., D:] * (y_moba @ w_branch_moba[i]))
        x = x + merged @ w_out[i]
        hm = rms_norm(x, g_mlp[i])
        x = x + jnp.square(jax.nn.relu(hm @ w_mlp_in[i])) @ w_mlp_out[i]
        hp = rms_norm(x, g_ple[i])
        x = x + jax.nn.sigmoid(hp @ w_ple_gate[i]) * (p[i] @ w_ple_up[i])
    return rms_norm(x, g_final)
```

```python
import functools
import math

import jax
import jax.numpy as jnp
from jax import lax
from jax.experimental import pallas as pl
from jax.experimental.pallas import tpu as pltpu

F32 = jnp.float32
BF16 = jnp.bfloat16

NORM_EPS = 1e-6
GN_EPS = 64e-5
NEG_INF = -1e30
LOG2E = math.log2(math.e)

LANES = 128
RWKV_HEAD_DIM = 64
RWKV_CHUNK = 64
RWKV_STEP = 256
MOBA_HEAD_DIM = 128
MOBA_BLOCK = 256
MOBA_TOPK = 3
LORA_PAD = 128
VMEM_LIMIT = 48 * 1024 * 1024


def _cparams(sem):
    return pltpu.CompilerParams(dimension_semantics=sem, vmem_limit_bytes=VMEM_LIMIT)


def _rms(xf, g):
    return xf * lax.rsqrt(jnp.mean(xf * xf, axis=-1, keepdims=True) + NORM_EPS) * g


def _dot(a, b):
    return jnp.dot(a, b, preferred_element_type=F32)


def _dot_nt(a, b):
    return lax.dot_general(a, b, (((1,), (1,)), ((), ())), preferred_element_type=F32)


def _dot_tn(a, b):
    return lax.dot_general(a, b, (((0,), (0,)), ((), ())), preferred_element_type=F32)


def _inproj_kernel(x_ref, g_ref, w_ref, o_ref, h_ref, *, act):
    @pl.when(pl.program_id(1) == 0)
    def _():
        h_ref[...] = _rms(x_ref[...], g_ref[...]).astype(BF16)

    z = _dot(h_ref[...], w_ref[...])
    if act == "sigmoid":
        z = jax.nn.sigmoid(z)
    o_ref[...] = z.astype(o_ref.dtype)


def _inproj(x2, g, w, *, act, out_dtype, tm, tn):
    m, d = x2.shape
    n = w.shape[1]
    return pl.pallas_call(
        functools.partial(_inproj_kernel, act=act),
        grid=(m // tm, n // tn),
        in_specs=[pl.BlockSpec((tm, d), lambda i, j: (i, 0)),
                  pl.BlockSpec((1, d), lambda i, j: (0, 0)),
                  pl.BlockSpec((d, tn), lambda i, j: (0, j))],
        out_specs=pl.BlockSpec((tm, tn), lambda i, j: (i, j)),
        out_shape=jax.ShapeDtypeStruct((m, n), out_dtype),
        scratch_shapes=[pltpu.VMEM((tm, d), BF16)],
        compiler_params=_cparams(("arbitrary", "arbitrary")),
        name="inproj_" + act,
    )(x2, g, w)


def _head_sum(x, lane_lo):
    s0 = jnp.sum(jnp.where(lane_lo, x, 0.0), axis=-1, keepdims=True)
    s1 = jnp.sum(jnp.where(lane_lo, 0.0, x), axis=-1, keepdims=True)
    return jnp.where(lane_lo, s0, s1)


def _rwkv_kernel(r_ref, k_ref, v_ref, wd_ref, ad_ref, gd_ref, pv_ref, mul_ref,
                 wdu_ref, wau_ref, wgu_ref, o_ref, s_ref, prev_ref, prevl_ref):
    C, T, N = RWKV_CHUNK, RWKV_STEP, RWKV_HEAD_DIM
    t_idx = pl.program_id(2)

    @pl.when(t_idx == 0)
    def _():
        s_ref[...] = jnp.zeros_like(s_ref)
        prev_ref[...] = jnp.zeros_like(prev_ref)
        prevl_ref[...] = jnp.zeros_like(prevl_ref)

    row_t = lax.broadcasted_iota(jnp.int32, (T, 1), 0)

    def shift(z, prev_row, mu):
        zp = jnp.where(row_t == 0, prev_row, pltpu.roll(z, 1, 0))
        return z + mu * (zp - z)

    pv = pv_ref[...]
    w0, a0, k_k, k_a, r_k, gn_w, gn_b = (pv[i:i + 1] for i in range(7))
    mu_r, mu_k, mu_v = pv[7:8], pv[8:9], pv[9:10]
    mul = mul_ref[...]

    zr, zk, zv = r_ref[...], k_ref[...], v_ref[...]
    zwd, zad, zgd = wd_ref[...], ad_ref[...], gd_ref[...]
    r = shift(zr, prev_ref[0:1, :], mu_r)
    k = shift(zk, prev_ref[1:2, :], mu_k)
    v = shift(zv, prev_ref[2:3, :], mu_v)
    wd = shift(zwd, prevl_ref[0:1, 0:LORA_PAD], mul[:, 0:LORA_PAD])
    ad = shift(zad, prevl_ref[0:1, LORA_PAD:2 * LORA_PAD], mul[:, LORA_PAD:2 * LORA_PAD])
    gd = shift(zgd, prevl_ref[0:1, 2 * LORA_PAD:], mul[:, 2 * LORA_PAD:])
    prev_ref[0:1, :] = zr[T - 1:T]
    prev_ref[1:2, :] = zk[T - 1:T]
    prev_ref[2:3, :] = zv[T - 1:T]
    prevl_ref[0:1, 0:LORA_PAD] = zwd[T - 1:T]
    prevl_ref[0:1, LORA_PAD:2 * LORA_PAD] = zad[T - 1:T]
    prevl_ref[0:1, 2 * LORA_PAD:] = zgd[T - 1:T]

    w_log = -jax.nn.softplus(-(w0 + _dot(jnp.tanh(wd), wdu_ref[...]))) - 0.5
    lw = -jnp.exp(w_log)
    a = jax.nn.sigmoid(a0 + _dot(ad, wau_ref[...]))
    g = _dot(jax.nn.sigmoid(gd), wgu_ref[...])

    lane_lo = lax.broadcasted_iota(jnp.int32, (1, LANES), 1) < N
    kk = k * k_k
    kk = kk / jnp.maximum(jnp.sqrt(_head_sum(kk * kk, lane_lo)), 1e-12)
    kmod = k * (1.0 + (a - 1.0) * k_a)
    beta = kk * a
    bonus = _head_sum(r * kmod * r_k, lane_lo) * v

    row = lax.broadcasted_iota(jnp.int32, (2 * C, 2 * C), 0)
    col = lax.broadcasted_iota(jnp.int32, (2 * C, 2 * C), 1)
    same_head = (row // C) == (col // C)
    strict = same_head & (row > col)
    incl = same_head & (row >= col)
    eye = row == col
    tri = (lax.broadcasted_iota(jnp.int32, (C, C), 0)
           >= lax.broadcasted_iota(jnp.int32, (C, C), 1)).astype(F32)

    def stack(x):
        return jnp.concatenate([jnp.where(lane_lo, x, 0.0), jnp.where(lane_lo, 0.0, x)], axis=0)

    state = s_ref[...]
    for c in range(T // C):
        sl = slice(c * C, (c + 1) * C)
        lw_c = lw[sl]
        cum = jnp.dot(tri, lw_c, preferred_element_type=F32, precision=lax.Precision.HIGHEST)
        cum_end = cum[C - 1:C]
        e_cur = jnp.exp(cum)
        e_prev = jnp.exp(cum - lw_c)
        e_inv = jnp.exp(-cum)
        e_rem = jnp.exp(cum_end - cum)
        a_t = stack(-kk[sl] * e_prev)
        r_t = stack(r[sl] * e_cur)
        b_h = beta[sl] * e_inv
        k_h = kmod[sl] * e_inv
        b_t = stack(beta[sl] * e_rem)
        k_t = stack(kmod[sl] * e_rem)
        v_s = stack(v[sl])

        scores = _dot_nt(jnp.concatenate([a_t, r_t], axis=0),
                         jnp.concatenate([b_h, b_h, k_h, k_h], axis=0))
        a_ab = jnp.where(strict, scores[:2 * C, :2 * C], 0.0)
        a_ak = jnp.where(strict, scores[:2 * C, 2 * C:], 0.0)
        a_rb = jnp.where(incl, scores[2 * C:, :2 * C], 0.0)
        a_rk = jnp.where(incl, scores[2 * C:, 2 * C:], 0.0)

        inv = jnp.where(eye, 1.0, jnp.where((row // 2) == (col // 2), a_ab, 0.0))
        b = 2
        while b < C:
            lvl = jnp.where(((row // (2 * b)) == (col // (2 * b))) & ((row // b) != (col // b)),
                            a_ab, 0.0)
            inv = inv + _dot(inv, _dot(lvl, inv))
            b *= 2

        rhs = _dot(jnp.concatenate([a_t, a_ak], axis=1),
                   jnp.concatenate([state, v_s], axis=0))
        u = _dot(inv, rhs)
        y_s = _dot(jnp.concatenate([r_t, a_rb, a_rk], axis=1),
                   jnp.concatenate([state, u, v_s], axis=0))
        y = y_s[:C] + y_s[C:]
        g_col = jnp.sum(jnp.where(eye, jnp.exp(cum_end), 0.0), axis=1, keepdims=True)
        state = state * g_col + _dot_tn(jnp.concatenate([b_t, k_t], axis=0),
                                        jnp.concatenate([u, v_s], axis=0))

        mean = _head_sum(y, lane_lo) * (1.0 / N)
        yc = y - mean
        var = _head_sum(yc * yc, lane_lo) * (1.0 / N)
        yn = yc * lax.rsqrt(var + GN_EPS) * gn_w + gn_b
        o_ref[sl, :] = ((yn + bonus[sl]) * g[sl]).astype(o_ref.dtype)
    s_ref[...] = state


def _rwkv(z_r, pvec, mu_lora, wdu, wau, wgu, *, batch, seq):
    m = z_r.shape[0]
    width = wdu.shape[1]
    hp = width // LANES
    T = RWKV_STEP
    nt = seq // T
    tok = lambda b, h, t: b * nt + t
    gate_w = wgu.shape[0]
    return pl.pallas_call(
        _rwkv_kernel,
        grid=(batch, hp, nt),
        in_specs=[
            pl.BlockSpec((T, LANES), lambda b, h, t: (tok(b, h, t), h)),
            pl.BlockSpec((T, LANES), lambda b, h, t: (tok(b, h, t), hp + h)),
            pl.BlockSpec((T, LANES), lambda b, h, t: (tok(b, h, t), 2 * hp + h)),
            pl.BlockSpec((T, LORA_PAD), lambda b, h, t: (tok(b, h, t), 3 * hp)),
            pl.BlockSpec((T, LORA_PAD), lambda b, h, t: (tok(b, h, t), 3 * hp + 1)),
            pl.BlockSpec((T, gate_w), lambda b, h, t: (tok(b, h, t), (3 * hp + 2) * LANES // gate_w)),
            pl.BlockSpec((16, LANES), lambda b, h, t: (0, h)),
            pl.BlockSpec((1, 2 * LORA_PAD + gate_w), lambda b, h, t: (0, 0)),
            pl.BlockSpec((LORA_PAD, LANES), lambda b, h, t: (0, h)),
            pl.BlockSpec((LORA_PAD, LANES), lambda b, h, t: (0, h)),
            pl.BlockSpec((gate_w, LANES), lambda b, h, t: (0, h)),
        ],
        out_specs=pl.BlockSpec((T, LANES), lambda b, h, t: (tok(b, h, t), h)),
        out_shape=jax.ShapeDtypeStruct((m, width), BF16),
        scratch_shapes=[pltpu.VMEM((LANES, LANES), F32),
                        pltpu.VMEM((8, LANES), F32),
                        pltpu.VMEM((8, 2 * LORA_PAD + gate_w), F32)],
        compiler_params=_cparams(("arbitrary", "arbitrary", "arbitrary")),
        name="rwkv7_mix",
    )(z_r, z_r, z_r, z_r, z_r, z_r, pvec, mu_lora, wdu, wau, wgu)


def _moba_kernel(q_ref, k_ref, v_ref, slope_ref, o_ref, kb_ref, vt_ref, kmean_ref, selb_ref,
                 *, nb):
    BLK, DH = MOBA_BLOCK, MOBA_HEAD_DIM
    i = pl.program_id(2)
    nbp = kmean_ref.shape[0]

    @pl.when(i == 0)
    def _():
        kmean_ref[...] = jnp.zeros_like(kmean_ref)

        def prep(n, carry):
            kf = k_ref[pl.ds(pl.multiple_of(n * BLK, BLK), BLK), :]
            vf = v_ref[pl.ds(pl.multiple_of(n * BLK, BLK), BLK), :]
            kb_ref[n] = kf.astype(BF16)
            vt_ref[n] = vf.T.astype(BF16)
            kmean_ref[pl.ds(n, 1), :] = jnp.mean(kf, axis=0, keepdims=True)
            return carry

        lax.fori_loop(0, nb, prep, 0)

    q_t = q_ref[...].T
    gate = jnp.dot(kmean_ref[...], q_t, preferred_element_type=F32,
                   precision=lax.Precision.HIGHEST)
    blk_id = lax.broadcasted_iota(jnp.int32, (nbp, BLK), 0)
    past = blk_id < i
    gm = jnp.where(past, gate, NEG_INF)
    chosen = jnp.zeros((nbp, BLK), dtype=jnp.bool_)
    for _ in range(MOBA_TOPK):
        mx = jnp.max(gm, axis=0, keepdims=True)
        first = jnp.min(jnp.where(gm == mx, blk_id, nbp), axis=0, keepdims=True)
        pick = blk_id == first
        chosen = chosen | pick
        gm = jnp.where(pick, -jnp.inf, gm)
    selb_ref[...] = jnp.where(chosen & past, 0.0, NEG_INF * LOG2E)

    slope2 = slope_ref[...] * LOG2E
    q_s = (q_t * (DH ** -0.5 * LOG2E)).astype(BF16)
    dist = (lax.broadcasted_iota(jnp.int32, (BLK, BLK), 1)
            - lax.broadcasted_iota(jnp.int32, (BLK, BLK), 0))
    alibi = dist.astype(F32) * (-slope2)

    s = _dot(kb_ref[i], q_s) + jnp.where(dist >= 0, alibi, NEG_INF * LOG2E)
    m0 = jnp.max(s, axis=0, keepdims=True)
    p = jnp.exp2(s - m0)
    l0 = jnp.sum(p, axis=0, keepdims=True)
    acc0 = _dot(vt_ref[i], p.astype(BF16))

    def body(j, carry):
        m, l, acc = carry
        rb = selb_ref[pl.ds(j, 1), :] - slope2 * ((i - j) * BLK).astype(F32)
        s = _dot(kb_ref[j], q_s) + alibi + rb
        m_new = jnp.maximum(m, jnp.max(s, axis=0, keepdims=True))
        alpha = jnp.exp2(m - m_new)
        p = jnp.exp2(s - m_new)
        l = l * alpha + jnp.sum(p, axis=0, keepdims=True)
        acc = acc * alpha + _dot(vt_ref[j], p.astype(BF16))
        return m_new, l, acc

    _, l, acc = lax.fori_loop(0, i, body, (m0, l0, acc0))
    o_ref[...] = (acc / l).T.astype(o_ref.dtype)


def _moba(z_m, slopes, *, batch, seq):
    m, cols = z_m.shape
    heads = cols // (3 * MOBA_HEAD_DIM)
    BLK, DH = MOBA_BLOCK, MOBA_HEAD_DIM
    nb = seq // BLK
    nbp = -(-nb // 8) * 8
    return pl.pallas_call(
        functools.partial(_moba_kernel, nb=nb),
        grid=(batch, heads, nb),
        in_specs=[
            pl.BlockSpec((BLK, DH), lambda b, h, i: (b * nb + i, h)),
            pl.BlockSpec((seq, DH), lambda b, h, i: (b, heads + h)),
            pl.BlockSpec((seq, DH), lambda b, h, i: (b, 2 * heads + h)),
            pl.BlockSpec((None, 1, BLK), lambda b, h, i: (h, 0, 0)),
        ],
        out_specs=pl.BlockSpec((BLK, DH), lambda b, h, i: (b * nb + i, h)),
        out_shape=jax.ShapeDtypeStruct((m, heads * DH), BF16),
        scratch_shapes=[pltpu.VMEM((nb, BLK, DH), BF16),
                        pltpu.VMEM((nb, DH, BLK), BF16),
                        pltpu.VMEM((nbp, DH), F32),
                        pltpu.VMEM((nbp, BLK), F32)],
        compiler_params=_cparams(("arbitrary", "arbitrary", "arbitrary")),
        name="moba_attn",
    )(z_m, z_m, z_m, slopes)


def _merge_kernel(yr_ref, ym_ref, g1_ref, g2_ref, wr_ref, wm_ref, o_ref):
    br = _dot(yr_ref[...], wr_ref[...])
    bm = _dot(ym_ref[...], wm_ref[...])
    o_ref[...] = (g1_ref[...].astype(F32) * br + g2_ref[...].astype(F32) * bm).astype(o_ref.dtype)


def _merge(y_r, y_m, gates, w_r, w_m, *, tm, tn):
    m, kr = y_r.shape
    km = y_m.shape[1]
    d = w_r.shape[1]
    nj = d // tn
    return pl.pallas_call(
        _merge_kernel,
        grid=(m // tm, nj),
        in_specs=[pl.BlockSpec((tm, kr), lambda i, j: (i, 0)),
                  pl.BlockSpec((tm, km), lambda i, j: (i, 0)),
                  pl.BlockSpec((tm, tn), lambda i, j: (i, j)),
                  pl.BlockSpec((tm, tn), lambda i, j: (i, nj + j)),
                  pl.BlockSpec((kr, tn), lambda i, j: (0, j)),
                  pl.BlockSpec((km, tn), lambda i, j: (0, j))],
        out_specs=pl.BlockSpec((tm, tn), lambda i, j: (i, j)),
        out_shape=jax.ShapeDtypeStruct((m, d), BF16),
        compiler_params=_cparams(("arbitrary", "arbitrary")),
        name="branch_merge",
    )(y_r, y_m, gates, gates, w_r, w_m)


def _resid_mm_kernel(x_ref, a_ref, w_ref, o_ref):
    o_ref[...] = x_ref[...] + _dot(a_ref[...], w_ref[...])


def _resid_mm(x2, a, w, *, tm, tn):
    m, d = x2.shape
    k = a.shape[1]
    return pl.pallas_call(
        _resid_mm_kernel,
        grid=(m // tm, d // tn),
        in_specs=[pl.BlockSpec((tm, tn), lambda i, j: (i, j)),
                  pl.BlockSpec((tm, k), lambda i, j: (i, 0)),
                  pl.BlockSpec((k, tn), lambda i, j: (0, j))],
        out_specs=pl.BlockSpec((tm, tn), lambda i, j: (i, j)),
        out_shape=jax.ShapeDtypeStruct((m, d), F32),
        compiler_params=_cparams(("arbitrary", "arbitrary")),
        name="out_proj",
    )(x2, a, w)


def _mlp_kernel(x_ref, g_ref, w1_ref, w2_ref, o_ref, h_ref):
    @pl.when(pl.program_id(1) == 0)
    def _():
        xf = x_ref[...]
        h_ref[...] = _rms(xf, g_ref[...]).astype(BF16)
        o_ref[...] = xf

    u = jnp.maximum(_dot(h_ref[...], w1_ref[...]), 0.0)
    o_ref[...] += _dot((u * u).astype(BF16), w2_ref[...])


def _mlp(x2, g, w1, w2, *, tm, tf):
    m, d = x2.shape
    f = w1.shape[1]
    return pl.pallas_call(
        _mlp_kernel,
        grid=(m // tm, f // tf),
        in_specs=[pl.BlockSpec((tm, d), lambda i, j: (i, 0)),
                  pl.BlockSpec((1, d), lambda i, j: (0, 0)),
                  pl.BlockSpec((d, tf), lambda i, j: (0, j)),
                  pl.BlockSpec((tf, d), lambda i, j: (j, 0))],
        out_specs=pl.BlockSpec((tm, d), lambda i, j: (i, 0)),
        out_shape=jax.ShapeDtypeStruct((m, d), F32),
        scratch_shapes=[pltpu.VMEM((tm, d), BF16)],
        compiler_params=_cparams(("arbitrary", "arbitrary")),
        name="mlp_relu2",
    )(x2, g, w1, w2)


def _ple_kernel(x_ref, p_ref, g_ref, gf_ref, wg_ref, wu_ref, o_ref, *, final_norm):
    xf = x_ref[...]
    hp = _rms(xf, g_ref[...]).astype(BF16)
    gate = jax.nn.sigmoid(_dot(hp, wg_ref[...]))
    up = _dot(p_ref[...].astype(BF16), wu_ref[...])
    y = xf + gate * up
    o_ref[...] = _rms(y, gf_ref[...]) if final_norm else y


def _ple(x2, p2, g, gf, wg, wu, *, tm, final_norm):
    m, d = x2.shape
    pd = p2.shape[1]
    return pl.pallas_call(
        functools.partial(_ple_kernel, final_norm=final_norm),
        grid=(m // tm,),
        in_specs=[pl.BlockSpec((tm, d), lambda i: (i, 0)),
                  pl.BlockSpec((tm, pd), lambda i: (i, 0)),
                  pl.BlockSpec((1, d), lambda i: (0, 0)),
                  pl.BlockSpec((1, d), lambda i: (0, 0)),
                  pl.BlockSpec((d, d), lambda i: (0, 0)),
                  pl.BlockSpec((pd, d), lambda i: (0, 0))],
        out_specs=pl.BlockSpec((tm, d), lambda i: (i, 0)),
        out_shape=jax.ShapeDtypeStruct((m, d), F32),
        compiler_params=_cparams(("arbitrary",)),
        name="ple_final",
    )(x2, p2, g, gf, wg, wu)


def _pad_to(a, n, axis):
    pad = [(0, 0)] * a.ndim
    pad[axis] = (0, n - a.shape[axis])
    return jnp.pad(a, pad)


def kernel(x, p, g_mix, w_in, mu_shift, w0, w_decay_up, a0, w_aaa_up, w_gate_up, k_k, k_a, r_k,
           gn_w, gn_b, w_branch_rwkv, w_branch_moba, w_out, g_mlp, w_mlp_in, w_mlp_out, g_ple,
           w_ple_gate, w_ple_up, g_final):
    B, S, D = x.shape
    depth = w_in.shape[0]
    W = w0.shape[1]
    Wm = w_branch_moba.shape[1]
    dl, al, gl = w_decay_up.shape[1], w_aaa_up.shape[1], w_gate_up.shape[1]
    assert S % MOBA_BLOCK == 0 and S % RWKV_STEP == 0
    assert W % LANES == 0 and dl <= LORA_PAD and al <= LORA_PAD and gl % LANES == 0
    M = B * S
    tm = 512 if M % 512 == 0 else 256
    x2 = x.reshape(M, D)
    moba_heads = Wm // MOBA_HEAD_DIM
    slopes = 2.0 ** (-8.0 * (jnp.arange(moba_heads, dtype=F32) + 1.0) / moba_heads)
    slopes = jnp.broadcast_to(slopes[:, None, None], (moba_heads, 1, MOBA_BLOCK))

    for i in range(depth):
        wi = w_in[i]
        c0 = 3 * W
        c1, c2, c3 = c0 + dl, c0 + dl + al, c0 + dl + al + gl
        w_r = jnp.concatenate([wi[:, :c0], _pad_to(wi[:, c0:c1], LORA_PAD, 1),
                               _pad_to(wi[:, c1:c2], LORA_PAD, 1), wi[:, c2:c3]], axis=1).astype(BF16)
        w_m = wi[:, c3:c3 + 3 * Wm].astype(BF16)
        w_g = wi[:, c3 + 3 * Wm:].astype(BF16)
        mu = mu_shift[i]
        mu_lora = jnp.concatenate([_pad_to(mu[c0:c1], LORA_PAD, 0), _pad_to(mu[c1:c2], LORA_PAD, 0),
                                   mu[c2:c3]])[None, :]
        pvec = jnp.stack([w0[i], a0[i], k_k[i], k_a[i], r_k[i].reshape(-1), gn_w[i], gn_b[i],
                          mu[:W], mu[W:2 * W], mu[2 * W:3 * W]])
        pvec = _pad_to(pvec, 16, 0)
        wdu = _pad_to(w_decay_up[i], LORA_PAD, 0)
        wau = _pad_to(w_aaa_up[i], LORA_PAD, 0)
        gmix = g_mix[i][None, :]

        z_r = _inproj(x2, gmix, w_r, act="none", out_dtype=F32, tm=tm, tn=512)
        z_m = _inproj(x2, gmix, w_m, act="none", out_dtype=F32, tm=tm, tn=512)
        gates = _inproj(x2, gmix, w_g, act="sigmoid", out_dtype=BF16, tm=tm, tn=512)

        y_r = _rwkv(z_r, pvec, mu_lora, wdu, wau, w_gate_up[i], batch=B, seq=S)
        y_m = _moba(z_m, slopes, batch=B, seq=S)

        merged = _merge(y_r, y_m, gates, w_branch_rwkv[i].astype(BF16),
                        w_branch_moba[i].astype(BF16), tm=tm, tn=512)
        x2 = _resid_mm(x2, merged, w_out[i].astype(BF16), tm=tm, tn=512)
        x2 = _mlp(x2, g_mlp[i][None, :], w_mlp_in[i].astype(BF16), w_mlp_out[i].astype(BF16),
                  tm=tm, tf=512)
        x2 = _ple(x2, p[i].reshape(M, -1), g_ple[i][None, :], g_final[None, :],
                  w_ple_gate[i].astype(BF16), w_ple_up[i].astype(BF16), tm=256,
                  final_norm=(i == depth - 1))
    return x2.reshape(B, S, D)
```

```python
import functools
import math

import jax
import jax.numpy as jnp
from jax import lax
from jax.experimental import pallas as pl
from jax.experimental.pallas import tpu as pltpu

F32 = jnp.float32
BF16 = jnp.bfloat16

NORM_EPS = 1e-6
GN_EPS = 64e-5
NEG_INF = -1e30
LOG2E = math.log2(math.e)

LANES = 128
SUBLANES = 8
RWKV_HEAD_DIM = 64
RWKV_CHUNK = 64
RWKV_STEP = 256
MOBA_HEAD_DIM = 128
MOBA_BLOCK = 256
MOBA_TOPK = 3
MOBA_GROUP = 4
LORA_PAD = 128
VMEM_LIMIT = 56 * 1024 * 1024


def _cparams(sem):
    return pltpu.CompilerParams(dimension_semantics=sem, vmem_limit_bytes=VMEM_LIMIT)


def _rms(xf, g):
    return xf * lax.rsqrt(jnp.mean(xf * xf, axis=-1, keepdims=True) + NORM_EPS) * g


def _dot(a, b):
    return jnp.dot(a.astype(BF16), b.astype(BF16), preferred_element_type=F32)


def _dot_nt(a, b):
    return lax.dot_general(a.astype(BF16), b.astype(BF16), (((1,), (1,)), ((), ())),
                           preferred_element_type=F32)


def _dot_tn(a, b):
    return lax.dot_general(a.astype(BF16), b.astype(BF16), (((0,), (0,)), ((), ())),
                           preferred_element_type=F32)


def _inproj_kernel(x_ref, g_ref, w_ref, o_ref, h_ref, *, act):
    @pl.when(pl.program_id(1) == 0)
    def _():
        h_ref[...] = _rms(x_ref[...], g_ref[...]).astype(BF16)

    z = _dot(h_ref[...], w_ref[...])
    if act == "sigmoid":
        z = jax.nn.sigmoid(z)
    o_ref[...] = z.astype(o_ref.dtype)


def _col_tiles(w, tn):
    k, n = w.shape
    return w.astype(BF16).reshape(k, n // tn, tn).transpose(1, 0, 2)


def _inproj(x2, g, w, *, act, out_dtype, tm):
    m, d = x2.shape
    nt, _, tn = w.shape
    n = nt * tn
    return pl.pallas_call(
        functools.partial(_inproj_kernel, act=act),
        grid=(m // tm, nt),
        in_specs=[pl.BlockSpec((tm, d), lambda i, j: (i, 0)),
                  pl.BlockSpec((1, d), lambda i, j: (0, 0)),
                  pl.BlockSpec((None, d, tn), lambda i, j: (j, 0, 0))],
        out_specs=pl.BlockSpec((tm, tn), lambda i, j: (i, j)),
        out_shape=jax.ShapeDtypeStruct((m, n), out_dtype),
        scratch_shapes=[pltpu.VMEM((tm, d), BF16)],
        compiler_params=_cparams(("arbitrary", "arbitrary")),
        name="inproj_" + act,
    )(x2, g, w)


def _head_sum(x, lane_lo):
    s0 = jnp.sum(jnp.where(lane_lo, x, 0.0), axis=-1, keepdims=True)
    s1 = jnp.sum(jnp.where(lane_lo, 0.0, x), axis=-1, keepdims=True)
    return jnp.where(lane_lo, s0, s1)


def _rwkv_kernel(r_ref, k_ref, v_ref, wd_ref, ad_ref, gd_ref, pv_ref, mul_ref,
                 wdu_ref, wau_ref, wgu_ref, o_ref, s_ref, prev_ref, prevl_ref):
    C, T, N = RWKV_CHUNK, RWKV_STEP, RWKV_HEAD_DIM
    nbatch = r_ref.shape[0]
    nchunk = T // C

    @pl.when(pl.program_id(1) == 0)
    def _():
        s_ref[...] = jnp.zeros_like(s_ref)
        prev_ref[...] = jnp.zeros_like(prev_ref)
        prevl_ref[...] = jnp.zeros_like(prevl_ref)

    row_t = lax.broadcasted_iota(jnp.int32, (T, 1), 0)

    def shift(z, prev_row, mu):
        zp = jnp.where(row_t == 0, prev_row, pltpu.roll(z, 1, 0))
        return z + mu * (zp - z)

    pv = pv_ref[...]
    w0, a0, k_k, k_a, r_k, gn_w, gn_b = (pv[i:i + 1] for i in range(7))
    mu_r, mu_k, mu_v = pv[7:8], pv[8:9], pv[9:10]
    mul = mul_ref[...]
    lane_lo = lax.broadcasted_iota(jnp.int32, (1, LANES), 1) < N

    row = lax.broadcasted_iota(jnp.int32, (2 * C, 2 * C), 0)
    col = lax.broadcasted_iota(jnp.int32, (2 * C, 2 * C), 1)
    same_head = (row // C) == (col // C)
    strict = same_head & (row > col)
    incl = same_head & (row >= col)
    eye = row == col
    pair = (row // 2) == (col // 2)
    levels = []
    b = 2
    while b < C:
        levels.append(((row // (2 * b)) == (col // (2 * b))) & ((row // b) != (col // b)))
        b *= 2
    tri = (lax.broadcasted_iota(jnp.int32, (C, C), 0)
           >= lax.broadcasted_iota(jnp.int32, (C, C), 1)).astype(F32)

    def stack(x):
        return jnp.concatenate([jnp.where(lane_lo, x, 0.0), jnp.where(lane_lo, 0.0, x)], axis=0)

    units = []
    for bi in range(nbatch):
        zr, zk, zv = r_ref[bi], k_ref[bi], v_ref[bi]
        zwd, zad, zgd = wd_ref[bi], ad_ref[bi], gd_ref[bi]
        pr = prev_ref[bi]
        pl_ = prevl_ref[bi]
        r = shift(zr, pr[0:1], mu_r)
        k = shift(zk, pr[1:2], mu_k)
        v = shift(zv, pr[2:3], mu_v)
        wd = shift(zwd, pl_[0:1, 0:LORA_PAD], mul[:, 0:LORA_PAD])
        ad = shift(zad, pl_[0:1, LORA_PAD:2 * LORA_PAD], mul[:, LORA_PAD:2 * LORA_PAD])
        gd = shift(zgd, pl_[0:1, 2 * LORA_PAD:], mul[:, 2 * LORA_PAD:])
        prev_ref[bi, 0:1, :] = zr[T - 1:T]
        prev_ref[bi, 1:2, :] = zk[T - 1:T]
        prev_ref[bi, 2:3, :] = zv[T - 1:T]
        prevl_ref[bi, 0:1, 0:LORA_PAD] = zwd[T - 1:T]
        prevl_ref[bi, 0:1, LORA_PAD:2 * LORA_PAD] = zad[T - 1:T]
        prevl_ref[bi, 0:1, 2 * LORA_PAD:] = zgd[T - 1:T]

        w_log = -jax.nn.softplus(-(w0 + _dot(jnp.tanh(wd), wdu_ref[...]))) - 0.5
        lw = -jnp.exp(w_log)
        a = jax.nn.sigmoid(a0 + _dot(ad, wau_ref[...]))
        g = _dot(jax.nn.sigmoid(gd), wgu_ref[...])
        kk = k * k_k
        kk = kk / jnp.maximum(jnp.sqrt(_head_sum(kk * kk, lane_lo)), 1e-12)
        kmod = k * (1.0 + (a - 1.0) * k_a)
        beta = kk * a
        bonus = _head_sum(r * kmod * r_k, lane_lo) * v

        for c in range(nchunk):
            sl = slice(c * C, (c + 1) * C)
            units.append(dict(bi=bi, sl=sl, lw=lw[sl], kk=kk[sl], r=r[sl], beta=beta[sl],
                              kmod=kmod[sl], v=v[sl], bonus=bonus[sl], g=g[sl]))

    for un in units:
        un["cum"] = jnp.dot(tri, un["lw"], preferred_element_type=F32,
                            precision=lax.Precision.HIGHEST)
    for un in units:
        cum, lw_c = un["cum"], un["lw"]
        cum_end = cum[C - 1:C]
        e_inv = jnp.exp(-cum)
        e_rem = jnp.exp(cum_end - cum)
        un["a_t"] = stack(-un["kk"] * jnp.exp(cum - lw_c))
        un["r_t"] = stack(un["r"] * jnp.exp(cum))
        un["b_h"] = un["beta"] * e_inv
        un["k_h"] = un["kmod"] * e_inv
        un["b_t"] = stack(un["beta"] * e_rem)
        un["k_t"] = stack(un["kmod"] * e_rem)
        un["v_s"] = stack(un["v"])
        un["g_col"] = jnp.sum(jnp.where(eye, jnp.exp(cum_end), 0.0), axis=1, keepdims=True)
    for un in units:
        un["scores"] = _dot_nt(jnp.concatenate([un["a_t"], un["r_t"]], axis=0),
                               jnp.concatenate([un["b_h"], un["b_h"], un["k_h"], un["k_h"]],
                                               axis=0))
    for un in units:
        sc = un.pop("scores")
        un["a_ab"] = jnp.where(strict, sc[:2 * C, :2 * C], 0.0)
        un["a_ak"] = jnp.where(strict, sc[:2 * C, 2 * C:], 0.0)
        un["a_rb"] = jnp.where(incl, sc[2 * C:, :2 * C], 0.0)
        un["a_rk"] = jnp.where(incl, sc[2 * C:, 2 * C:], 0.0)
        un["inv"] = jnp.where(eye, 1.0, jnp.where(pair, un["a_ab"], 0.0))
    for un in units:
        un["av"] = _dot(jnp.concatenate([un["a_ak"], un["a_rk"]], axis=0), un["v_s"])
    for un in units:
        un["sv"] = _dot_tn(un["k_t"], un["v_s"])
    for lvl in levels:
        for un in units:
            un["tmp"] = _dot(jnp.where(lvl, un["a_ab"], 0.0), un["inv"])
        for un in units:
            un["inv"] = un["inv"] + _dot(un["inv"], un.pop("tmp"))
    for un in units:
        wu = _dot(un["inv"], jnp.concatenate([un["a_t"], un["av"][:2 * C]], axis=1))
        un["w"], un["u0"], un["y0"] = wu[:, :LANES], wu[:, LANES:], un["av"][2 * C:]

    states = [s_ref[bi] for bi in range(nbatch)]
    for c in range(nchunk):
        cur = [units[bi * nchunk + c] for bi in range(nbatch)]
        us = [un["u0"] + _dot(un["w"], st) for un, st in zip(cur, states)]
        ys = [un["y0"] + _dot(jnp.concatenate([un["r_t"], un["a_rb"]], axis=1),
                              jnp.concatenate([st, u], axis=0))
              for un, st, u in zip(cur, states, us)]
        states = [st * un["g_col"] + un["sv"] + _dot_tn(un["b_t"], u)
                  for un, st, u in zip(cur, states, us)]
        for un, y_s in zip(cur, ys):
            y = y_s[:C] + y_s[C:]
            mean = _head_sum(y, lane_lo) * (1.0 / N)
            yc = y - mean
            var = _head_sum(yc * yc, lane_lo) * (1.0 / N)
            yn = yc * lax.rsqrt(var + GN_EPS) * gn_w + gn_b
            o_ref[un["bi"], un["sl"], :] = ((yn + un["bonus"]) * un["g"]).astype(o_ref.dtype)
    for bi in range(nbatch):
        s_ref[bi] = states[bi]


def _rwkv(z_r, pvec, mu_lora, wdu, wau, wgu):
    batch, seq, _ = z_r.shape
    width = wdu.shape[1]
    hp = width // LANES
    T = RWKV_STEP
    gate_w = wgu.shape[0]
    lora_w = 2 * LORA_PAD + gate_w
    return pl.pallas_call(
        _rwkv_kernel,
        grid=(hp, seq // T),
        in_specs=[
            pl.BlockSpec((batch, T, LANES), lambda h, t: (0, t, h)),
            pl.BlockSpec((batch, T, LANES), lambda h, t: (0, t, hp + h)),
            pl.BlockSpec((batch, T, LANES), lambda h, t: (0, t, 2 * hp + h)),
            pl.BlockSpec((batch, T, LORA_PAD), lambda h, t: (0, t, 3 * hp)),
            pl.BlockSpec((batch, T, LORA_PAD), lambda h, t: (0, t, 3 * hp + 1)),
            pl.BlockSpec((batch, T, gate_w), lambda h, t: (0, t, (3 * hp + 2) * LANES // gate_w)),
            pl.BlockSpec((16, LANES), lambda h, t: (0, h)),
            pl.BlockSpec((1, lora_w), lambda h, t: (0, 0)),
            pl.BlockSpec((LORA_PAD, LANES), lambda h, t: (0, h)),
            pl.BlockSpec((LORA_PAD, LANES), lambda h, t: (0, h)),
            pl.BlockSpec((gate_w, LANES), lambda h, t: (0, h)),
        ],
        out_specs=pl.BlockSpec((batch, T, LANES), lambda h, t: (0, t, h)),
        out_shape=jax.ShapeDtypeStruct((batch, seq, width), BF16),
        scratch_shapes=[pltpu.VMEM((batch, LANES, LANES), F32),
                        pltpu.VMEM((batch, SUBLANES, LANES), F32),
                        pltpu.VMEM((batch, SUBLANES, lora_w), F32)],
        compiler_params=_cparams(("arbitrary", "arbitrary")),
        name="rwkv7_mix",
    )(z_r, z_r, z_r, z_r, z_r, z_r, pvec, mu_lora, wdu, wau, wgu)


def _split3(c):
    hi = c.astype(BF16).astype(F32)
    mid = (c - hi).astype(BF16).astype(F32)
    lo = (c - hi - mid).astype(BF16).astype(F32)
    return hi, mid, lo


def _moba_kernel(q_ref, k_ref, v_ref, slope_ref, o_ref, ka_ref, vt_ref, kmean_ref, s_ref,
                 *, nb, group):
    BLK, DH = MOBA_BLOCK, MOBA_HEAD_DIM
    i = pl.program_id(2)
    nbp = kmean_ref.shape[0]
    neg = NEG_INF * LOG2E

    @pl.when(i == 0)
    def _():
        kmean_ref[...] = jnp.zeros_like(kmean_ref)
        lane = lax.broadcasted_iota(jnp.int32, (BLK, DH), 1)
        key_off = lax.broadcasted_iota(jnp.int32, (BLK, DH), 0).astype(F32)

        def prep(n, carry):
            kf = k_ref[pl.ds(pl.multiple_of(n * BLK, BLK), BLK), :]
            vf = v_ref[pl.ds(pl.multiple_of(n * BLK, BLK), BLK), :]
            ext = jnp.where(lane < nbp, jnp.where(lane == n, 1.0, 0.0),
                            jnp.where(lane < nbp + 3, n.astype(F32),
                                      jnp.where(lane < nbp + 6, key_off, 0.0)))
            ka_ref[n] = jnp.concatenate([kf, ext], axis=1).astype(BF16)
            vt_ref[n] = vf.T.astype(BF16)
            kmean_ref[pl.ds(n, 1), :] = jnp.mean(kf, axis=0, keepdims=True)
            return carry

        lax.fori_loop(0, nb, prep, 0)

    q_t = q_ref[...].T
    gate = jnp.dot(kmean_ref[...], q_t, preferred_element_type=F32,
                   precision=lax.Precision.HIGHEST)
    blk_id = lax.broadcasted_iota(jnp.int32, (nbp, BLK), 0)
    past = blk_id < i
    gm = jnp.where(past, gate, NEG_INF)
    chosen = jnp.zeros((nbp, BLK), dtype=jnp.bool_)
    for _ in range(MOBA_TOPK):
        mx = jnp.max(gm, axis=0, keepdims=True)
        first = jnp.min(jnp.where(gm == mx, blk_id, nbp), axis=0, keepdims=True)
        pick = blk_id == first
        chosen = chosen | pick
        gm = jnp.where(pick, -jnp.inf, gm)
    selb = jnp.where(chosen & past, 0.0, neg)

    slope2 = slope_ref[...] * LOG2E
    coef = _split3(slope2 * float(BLK)) + _split3(slope2)
    erow = lax.broadcasted_iota(jnp.int32, (DH - nbp, BLK), 0)
    ext = jnp.zeros((DH - nbp, BLK), F32)
    for idx, cf in enumerate(coef):
        ext = jnp.where(erow == idx, cf, ext)
    q_s = q_t * (DH ** -0.5 * LOG2E)
    q_own = jnp.concatenate([q_s, jnp.zeros((nbp, BLK), F32), ext], axis=0).astype(BF16)
    q_past = jnp.concatenate([q_s, selb, ext], axis=0).astype(BF16)

    causal = (lax.broadcasted_iota(jnp.int32, (BLK, BLK), 1)
              >= lax.broadcasted_iota(jnp.int32, (BLK, BLK), 0))
    s = jnp.where(causal, _dot(ka_ref[i], q_own), neg)
    m0 = jnp.max(s, axis=0, keepdims=True)
    p = jnp.exp2(s - m0)
    l0 = jnp.sum(p, axis=0, keepdims=True)
    acc0 = _dot(vt_ref[i], p.astype(BF16))

    n_groups = (i + group - 1) // group

    last_group = nb // group - 1

    def logits_to(slot, gidx):
        mx = None
        for t in range(group):
            s = _dot(ka_ref[gidx * group + t], q_past)
            s_ref[slot, t] = s
            smax = jnp.max(s, axis=0, keepdims=True)
            mx = smax if mx is None else jnp.maximum(mx, smax)
        return mx

    def half_trip(slot, gidx, state, mx_next_of):
        m, l, acc, mx = state
        mx_next = logits_to(1 - slot, mx_next_of)
        m_new = jnp.maximum(m, mx)
        alpha = jnp.exp2(m - m_new)
        l = l * alpha
        acc = acc * alpha
        for t in range(group):
            p = jnp.exp2(s_ref[slot, t] - m_new)
            l = l + jnp.sum(p, axis=0, keepdims=True)
            acc = acc + _dot(vt_ref[gidx * group + t], p.astype(BF16))
        return m_new, l, acc, mx_next

    def body(pi, carry):
        g0 = 2 * pi
        carry = half_trip(0, g0, carry, g0 + 1)
        return half_trip(1, g0 + 1, carry, jnp.minimum(g0 + 2, last_group))

    mx0 = logits_to(0, 0)
    _, l, acc, _ = lax.fori_loop(0, (n_groups + 1) // 2, body, (m0, l0, acc0, mx0))
    o_ref[...] = (acc / l).T.astype(o_ref.dtype)


def _moba(z_m, slopes, *, batch, seq):
    m, cols = z_m.shape
    heads = cols // (3 * MOBA_HEAD_DIM)
    BLK, DH = MOBA_BLOCK, MOBA_HEAD_DIM
    nb = seq // BLK
    nbp = -(-nb // SUBLANES) * SUBLANES
    assert nbp + 6 <= DH
    group = next(g for g in (MOBA_GROUP, 2, 1) if g == 1 or nb % (2 * g) == 0)
    return pl.pallas_call(
        functools.partial(_moba_kernel, nb=nb, group=group),
        grid=(batch, heads, nb),
        in_specs=[
            pl.BlockSpec((BLK, DH), lambda b, h, i: (b * nb + i, h)),
            pl.BlockSpec((seq, DH), lambda b, h, i: (b, heads + h)),
            pl.BlockSpec((seq, DH), lambda b, h, i: (b, 2 * heads + h)),
            pl.BlockSpec((None, 1, BLK), lambda b, h, i: (h, 0, 0)),
        ],
        out_specs=pl.BlockSpec((BLK, DH), lambda b, h, i: (b * nb + i, h)),
        out_shape=jax.ShapeDtypeStruct((m, heads * DH), BF16),
        scratch_shapes=[pltpu.VMEM((nb, BLK, 2 * DH), BF16),
                        pltpu.VMEM((nb, DH, BLK), BF16),
                        pltpu.VMEM((nbp, DH), F32),
                        pltpu.VMEM((2, group, BLK, BLK), F32)],
        compiler_params=_cparams(("arbitrary", "arbitrary", "arbitrary")),
        name="moba_attn",
    )(z_m, z_m, z_m, slopes)


def _merge_kernel(yr_ref, ym_ref, g1_ref, g2_ref, wr_ref, wm_ref, o_ref):
    br = _dot(yr_ref[...], wr_ref[...])
    bm = _dot(ym_ref[...], wm_ref[...])
    o_ref[...] = (g1_ref[...].astype(F32) * br + g2_ref[...].astype(F32) * bm).astype(o_ref.dtype)


def _merge(y_r, y_m, gates, w_r, w_m, *, tm):
    m, kr = y_r.shape
    km = y_m.shape[1]
    nj, _, tn = w_r.shape
    d = nj * tn
    return pl.pallas_call(
        _merge_kernel,
        grid=(m // tm, nj),
        in_specs=[pl.BlockSpec((tm, kr), lambda i, j: (i, 0)),
                  pl.BlockSpec((tm, km), lambda i, j: (i, 0)),
                  pl.BlockSpec((tm, tn), lambda i, j: (i, j)),
                  pl.BlockSpec((tm, tn), lambda i, j: (i, nj + j)),
                  pl.BlockSpec((None, kr, tn), lambda i, j: (j, 0, 0)),
                  pl.BlockSpec((None, km, tn), lambda i, j: (j, 0, 0))],
        out_specs=pl.BlockSpec((tm, tn), lambda i, j: (i, j)),
        out_shape=jax.ShapeDtypeStruct((m, d), BF16),
        compiler_params=_cparams(("arbitrary", "arbitrary")),
        name="branch_merge",
    )(y_r, y_m, gates, gates, w_r, w_m)


def _resid_mm_kernel(x_ref, a_ref, w_ref, o_ref):
    o_ref[...] = x_ref[...] + _dot(a_ref[...], w_ref[...])


def _resid_mm(x2, a, w, *, tm):
    m, d = x2.shape
    k = a.shape[1]
    nj, _, tn = w.shape
    return pl.pallas_call(
        _resid_mm_kernel,
        grid=(m // tm, nj),
        in_specs=[pl.BlockSpec((tm, tn), lambda i, j: (i, j)),
                  pl.BlockSpec((tm, k), lambda i, j: (i, 0)),
                  pl.BlockSpec((None, k, tn), lambda i, j: (j, 0, 0))],
        out_specs=pl.BlockSpec((tm, tn), lambda i, j: (i, j)),
        out_shape=jax.ShapeDtypeStruct((m, d), F32),
        compiler_params=_cparams(("arbitrary", "arbitrary")),
        name="out_proj",
    )(x2, a, w)


def _mlp_kernel(x_ref, g_ref, w1_ref, w2_ref, o_ref, h_ref):
    @pl.when(pl.program_id(1) == 0)
    def _():
        xf = x_ref[...]
        h_ref[...] = _rms(xf, g_ref[...]).astype(BF16)
        o_ref[...] = xf

    u = jnp.maximum(_dot(h_ref[...], w1_ref[...]), 0.0)
    o_ref[...] += _dot((u * u).astype(BF16), w2_ref[...])


def _mlp(x2, g, w1, w2, *, tm):
    m, d = x2.shape
    nf, _, tf = w1.shape
    return pl.pallas_call(
        _mlp_kernel,
        grid=(m // tm, nf),
        in_specs=[pl.BlockSpec((tm, d), lambda i, j: (i, 0)),
                  pl.BlockSpec((1, d), lambda i, j: (0, 0)),
                  pl.BlockSpec((None, d, tf), lambda i, j: (j, 0, 0)),
                  pl.BlockSpec((tf, d), lambda i, j: (j, 0))],
        out_specs=pl.BlockSpec((tm, d), lambda i, j: (i, 0)),
        out_shape=jax.ShapeDtypeStruct((m, d), F32),
        scratch_shapes=[pltpu.VMEM((tm, d), BF16)],
        compiler_params=_cparams(("arbitrary", "arbitrary")),
        name="mlp_relu2",
    )(x2, g, w1, w2)


def _ple_kernel(x_ref, p_ref, g_ref, gf_ref, wg_ref, wu_ref, o_ref, *, final_norm):
    xf = x_ref[...]
    hp = _rms(xf, g_ref[...]).astype(BF16)
    gate = jax.nn.sigmoid(_dot(hp, wg_ref[...]))
    up = _dot(p_ref[...].astype(BF16), wu_ref[...])
    y = xf + gate * up
    o_ref[...] = _rms(y, gf_ref[...]) if final_norm else y


def _ple(x2, p2, g, gf, wg, wu, *, tm, final_norm):
    m, d = x2.shape
    pd = p2.shape[1]
    return pl.pallas_call(
        functools.partial(_ple_kernel, final_norm=final_norm),
        grid=(m // tm,),
        in_specs=[pl.BlockSpec((tm, d), lambda i: (i, 0)),
                  pl.BlockSpec((tm, pd), lambda i: (i, 0)),
                  pl.BlockSpec((1, d), lambda i: (0, 0)),
                  pl.BlockSpec((1, d), lambda i: (0, 0)),
                  pl.BlockSpec((d, d), lambda i: (0, 0)),
                  pl.BlockSpec((pd, d), lambda i: (0, 0))],
        out_specs=pl.BlockSpec((tm, d), lambda i: (i, 0)),
        out_shape=jax.ShapeDtypeStruct((m, d), F32),
        compiler_params=_cparams(("arbitrary",)),
        name="ple_final",
    )(x2, p2, g, gf, wg, wu)


def _tile_plan(m):
    def rows(cap):
        return next(t for t in (1024, 512, 256, 128, 64, 32, 16, 8) if t <= cap and m % t == 0)

    return dict(tn=512, tm_in=rows(1024), tm_out=rows(1024), tm_mlp=rows(1024), tm_ple=rows(256))


def _pad_to(a, n, axis):
    pad = [(0, 0)] * a.ndim
    pad[axis] = (0, n - a.shape[axis])
    return jnp.pad(a, pad)


def kernel(x, p, g_mix, w_in, mu_shift, w0, w_decay_up, a0, w_aaa_up, w_gate_up, k_k, k_a, r_k,
           gn_w, gn_b, w_branch_rwkv, w_branch_moba, w_out, g_mlp, w_mlp_in, w_mlp_out, g_ple,
           w_ple_gate, w_ple_up, g_final):
    B, S, D = x.shape
    depth = w_in.shape[0]
    W = w0.shape[1]
    Wm = w_branch_moba.shape[1]
    dl, al, gl = w_decay_up.shape[1], w_aaa_up.shape[1], w_gate_up.shape[1]
    assert S % MOBA_BLOCK == 0 and S % RWKV_STEP == 0
    assert W % LANES == 0 and dl <= LORA_PAD and al <= LORA_PAD and gl % LANES == 0
    M = B * S
    tiles = _tile_plan(M)
    x2 = x.reshape(M, D)
    moba_heads = Wm // MOBA_HEAD_DIM
    slopes = 2.0 ** (-8.0 * (jnp.arange(moba_heads, dtype=F32) + 1.0) / moba_heads)
    slopes = jnp.broadcast_to(slopes[:, None, None], (moba_heads, 1, MOBA_BLOCK))

    for i in range(depth):
        wi = w_in[i]
        c0 = 3 * W
        c1, c2, c3 = c0 + dl, c0 + dl + al, c0 + dl + al + gl
        tn = tiles["tn"]
        w_r = _col_tiles(jnp.concatenate([wi[:, :c0], _pad_to(wi[:, c0:c1], LORA_PAD, 1),
                                          _pad_to(wi[:, c1:c2], LORA_PAD, 1), wi[:, c2:c3]],
                                         axis=1), tn)
        w_m = _col_tiles(wi[:, c3:c3 + 3 * Wm], tn)
        w_g = _col_tiles(wi[:, c3 + 3 * Wm:], tn)
        mu = mu_shift[i]
        mu_lora = jnp.concatenate([_pad_to(mu[c0:c1], LORA_PAD, 0), _pad_to(mu[c1:c2], LORA_PAD, 0),
                                   mu[c2:c3]])[None, :]
        pvec = jnp.stack([w0[i], a0[i], k_k[i], k_a[i], r_k[i].reshape(-1), gn_w[i], gn_b[i],
                          mu[:W], mu[W:2 * W], mu[2 * W:3 * W]])
        pvec = _pad_to(pvec, 16, 0)
        wdu = _pad_to(w_decay_up[i], LORA_PAD, 0)
        wau = _pad_to(w_aaa_up[i], LORA_PAD, 0)
        gmix = g_mix[i][None, :]

        z_r = _inproj(x2, gmix, w_r, act="none", out_dtype=F32, tm=tiles["tm_in"])
        z_m = _inproj(x2, gmix, w_m, act="none", out_dtype=F32, tm=tiles["tm_in"])
        gates = _inproj(x2, gmix, w_g, act="sigmoid", out_dtype=BF16, tm=tiles["tm_in"])

        y_r = _rwkv(z_r.reshape(B, S, -1), pvec, mu_lora, wdu, wau, w_gate_up[i]).reshape(M, W)
        y_m = _moba(z_m, slopes, batch=B, seq=S)

        merged = _merge(y_r, y_m, gates, _col_tiles(w_branch_rwkv[i], tn),
                        _col_tiles(w_branch_moba[i], tn), tm=tiles["tm_out"])
        x2 = _resid_mm(x2, merged, _col_tiles(w_out[i], tn), tm=tiles["tm_out"])
        x2 = _mlp(x2, g_mlp[i][None, :], _col_tiles(w_mlp_in[i], tn), w_mlp_out[i].astype(BF16),
                  tm=tiles["tm_mlp"])
        x2 = _ple(x2, p[i].reshape(M, -1), g_ple[i][None, :], g_final[None, :],
                  w_ple_gate[i].astype(BF16), w_ple_up[i].astype(BF16), tm=tiles["tm_ple"],
                  final_norm=(i == depth - 1))
    return x2.reshape(B, S, D)
```

```python
import functools
import math

import jax
import jax.numpy as jnp
from jax import lax
from jax.experimental import pallas as pl
from jax.experimental.pallas import tpu as pltpu

F32 = jnp.float32
BF16 = jnp.bfloat16

NORM_EPS = 1e-6
GN_EPS = 64e-5
NEG_INF = -1e30
LOG2E = math.log2(math.e)

LANES = 128
SUBLANES = 8
RWKV_HEAD_DIM = 64
RWKV_CHUNK = 64
RWKV_STEP = 64
RWKV_PAIRS = 4
MOBA_HEAD_DIM = 128
MOBA_BLOCK = 256
MOBA_TOPK = 3
MOBA_GROUP = 4
MOBA_ONES = 16
MOBA_UNROLL = 4
LORA_PAD = 128
VMEM_LIMIT = 56 * 1024 * 1024


def _cparams(sem):
    return pltpu.CompilerParams(dimension_semantics=sem, vmem_limit_bytes=VMEM_LIMIT)


def _rms(xf, g):
    return xf * lax.rsqrt(jnp.mean(xf * xf, axis=-1, keepdims=True) + NORM_EPS) * g


def _dot(a, b):
    return jnp.dot(a.astype(BF16), b.astype(BF16), preferred_element_type=F32)


def _dot_nt(a, b):
    return lax.dot_general(a.astype(BF16), b.astype(BF16), (((1,), (1,)), ((), ())),
                           preferred_element_type=F32)


def _dot_tn(a, b):
    return lax.dot_general(a.astype(BF16), b.astype(BF16), (((0,), (0,)), ((), ())),
                           preferred_element_type=F32)


def _norm_kernel(x_ref, g_ref, o_ref):
    o_ref[...] = _rms(x_ref[...], g_ref[...]).astype(o_ref.dtype)


def _norm(x2, g, *, tm):
    m, d = x2.shape
    return pl.pallas_call(
        _norm_kernel,
        grid=(m // tm,),
        in_specs=[pl.BlockSpec((tm, d), lambda i: (i, 0)),
                  pl.BlockSpec((1, d), lambda i: (0, 0))],
        out_specs=pl.BlockSpec((tm, d), lambda i: (i, 0)),
        out_shape=jax.ShapeDtypeStruct((m, d), BF16),
        compiler_params=_cparams(("arbitrary",)),
        name="mix_norm",
    )(x2, g)


def _inproj_kernel(h_ref, w_ref, o_ref, *, act):
    z = _dot(h_ref[...], w_ref[...])
    if act == "sigmoid":
        z = jax.nn.sigmoid(z)
    o_ref[...] = z.astype(o_ref.dtype)


def _col_tiles(w, tn):
    k, n = w.shape
    return w.astype(BF16).reshape(k, n // tn, tn).transpose(1, 0, 2)


def _inproj(h, w, *, act, out_dtype, tm):
    m, d = h.shape
    nt, _, tn = w.shape
    n = nt * tn
    return pl.pallas_call(
        functools.partial(_inproj_kernel, act=act),
        grid=(m // tm, nt),
        in_specs=[pl.BlockSpec((tm, d), lambda i, j: (i, 0)),
                  pl.BlockSpec((None, d, tn), lambda i, j: (j, 0, 0))],
        out_specs=pl.BlockSpec((tm, tn), lambda i, j: (i, j)),
        out_shape=jax.ShapeDtypeStruct((m, n), out_dtype),
        compiler_params=_cparams(("arbitrary", "arbitrary")),
        name="inproj_" + act,
    )(h, w)


def _head_sum(x, lane_lo):
    s0 = jnp.sum(jnp.where(lane_lo, x, 0.0), axis=-1, keepdims=True)
    s1 = jnp.sum(jnp.where(lane_lo, 0.0, x), axis=-1, keepdims=True)
    return jnp.where(lane_lo, s0, s1)


def _rwkv_kernel(r_ref, k_ref, v_ref, wd_ref, ad_ref, gd_ref, pv_ref, mul_ref,
                 wdu_ref, wau_ref, wgu_ref, o_ref, s_ref, prev_ref, prevl_ref):
    C, T, N = RWKV_CHUNK, RWKV_STEP, RWKV_HEAD_DIM
    nbatch = r_ref.shape[0]
    npair = r_ref.shape[2] // LANES
    nchunk = T // C

    @pl.when(pl.program_id(1) == 0)
    def _():
        s_ref[...] = jnp.zeros_like(s_ref)
        prev_ref[...] = jnp.zeros_like(prev_ref)
        prevl_ref[...] = jnp.zeros_like(prevl_ref)

    row_t = lax.broadcasted_iota(jnp.int32, (T, 1), 0)

    def shift(z, prev_row, mu):
        zp = jnp.where(row_t == 0, prev_row, pltpu.roll(z, 1, 0))
        return z + mu * (zp - z)

    pv = pv_ref[...]
    w0, a0, k_k, k_a, r_k, gn_w, gn_b = (pv[i:i + 1] for i in range(7))
    mu_r, mu_k, mu_v = pv[7:8], pv[8:9], pv[9:10]
    mul = mul_ref[...]
    lane_lo = lax.broadcasted_iota(jnp.int32, (1, LANES), 1) < N

    def head_sum(x):
        return jnp.concatenate([_head_sum(x[:, q * LANES:(q + 1) * LANES], lane_lo)
                                for q in range(npair)], axis=1)

    row = lax.broadcasted_iota(jnp.int32, (2 * C, 2 * C), 0)
    col = lax.broadcasted_iota(jnp.int32, (2 * C, 2 * C), 1)
    same_head = (row // C) == (col // C)
    strict = same_head & (row > col)
    incl = same_head & (row >= col)
    eye = row == col
    pair = (row // 2) == (col // 2)
    levels = []
    b = 2
    while b < C:
        levels.append(((row // (2 * b)) == (col // (2 * b))) & ((row // b) != (col // b)))
        b *= 2
    tri = (lax.broadcasted_iota(jnp.int32, (C, C), 0)
           >= lax.broadcasted_iota(jnp.int32, (C, C), 1)).astype(F32)

    def stack(x):
        return jnp.concatenate([jnp.where(lane_lo, x, 0.0), jnp.where(lane_lo, 0.0, x)], axis=0)

    units = []
    for bi in range(nbatch):
        zr, zk, zv = r_ref[bi], k_ref[bi], v_ref[bi]
        zwd, zad, zgd = wd_ref[bi], ad_ref[bi], gd_ref[bi]
        pr = prev_ref[bi]
        pl_ = prevl_ref[bi]
        r = shift(zr, pr[0:1], mu_r)
        k = shift(zk, pr[1:2], mu_k)
        v = shift(zv, pr[2:3], mu_v)
        wd = shift(zwd, pl_[0:1, 0:LORA_PAD], mul[:, 0:LORA_PAD])
        ad = shift(zad, pl_[0:1, LORA_PAD:2 * LORA_PAD], mul[:, LORA_PAD:2 * LORA_PAD])
        gd = shift(zgd, pl_[0:1, 2 * LORA_PAD:], mul[:, 2 * LORA_PAD:])
        prev_ref[bi, 0:1, :] = zr[T - 1:T]
        prev_ref[bi, 1:2, :] = zk[T - 1:T]
        prev_ref[bi, 2:3, :] = zv[T - 1:T]
        prevl_ref[bi, 0:1, 0:LORA_PAD] = zwd[T - 1:T]
        prevl_ref[bi, 0:1, LORA_PAD:2 * LORA_PAD] = zad[T - 1:T]
        prevl_ref[bi, 0:1, 2 * LORA_PAD:] = zgd[T - 1:T]

        w_log = -jax.nn.softplus(-(w0 + _dot(jnp.tanh(wd), wdu_ref[...]))) - 0.5
        lw = -jnp.exp(w_log)
        a = jax.nn.sigmoid(a0 + _dot(ad, wau_ref[...]))
        g = _dot(jax.nn.sigmoid(gd), wgu_ref[...])
        kk = k * k_k
        kk = kk / jnp.maximum(jnp.sqrt(head_sum(kk * kk)), 1e-12)
        kmod = k * (1.0 + (a - 1.0) * k_a)
        beta = kk * a
        bonus = head_sum(r * kmod * r_k) * v

        for q in range(npair):
            ln = slice(q * LANES, (q + 1) * LANES)
            for c in range(nchunk):
                sl = slice(c * C, (c + 1) * C)
                units.append(dict(bi=bi, q=q, c=c, sl=sl, ln=ln, lw=lw[sl, ln], kk=kk[sl, ln],
                                  r=r[sl, ln], beta=beta[sl, ln], kmod=kmod[sl, ln], v=v[sl, ln],
                                  bonus=bonus[sl, ln], g=g[sl, ln], gn_w=gn_w[:, ln],
                                  gn_b=gn_b[:, ln]))

    for un in units:
        un["cum"] = jnp.dot(tri, un["lw"], preferred_element_type=F32,
                            precision=lax.Precision.HIGHEST)
    for un in units:
        cum, lw_c = un["cum"], un["lw"]
        cum_end = cum[C - 1:C]
        e_inv = jnp.exp(-cum)
        e_rem = jnp.exp(cum_end - cum)
        un["a_t"] = stack(-un["kk"] * jnp.exp(cum - lw_c))
        un["r_t"] = stack(un["r"] * jnp.exp(cum))
        un["b_h"] = un["beta"] * e_inv
        un["k_h"] = un["kmod"] * e_inv
        un["b_t"] = stack(un["beta"] * e_rem)
        un["k_t"] = stack(un["kmod"] * e_rem)
        un["v_s"] = stack(un["v"])
        un["g_col"] = jnp.sum(jnp.where(eye, jnp.exp(cum_end), 0.0), axis=1, keepdims=True)
    for un in units:
        un["scores"] = _dot_nt(jnp.concatenate([un["a_t"], un["r_t"]], axis=0),
                               jnp.concatenate([un["b_h"], un["b_h"], un["k_h"], un["k_h"]],
                                               axis=0))
    for un in units:
        sc = un.pop("scores")
        un["a_ab"] = jnp.where(strict, sc[:2 * C, :2 * C], 0.0)
        un["a_ak"] = jnp.where(strict, sc[:2 * C, 2 * C:], 0.0)
        un["a_rb"] = jnp.where(incl, sc[2 * C:, :2 * C], 0.0)
        un["a_rk"] = jnp.where(incl, sc[2 * C:, 2 * C:], 0.0)
        un["inv"] = jnp.where(eye, 1.0, jnp.where(pair, un["a_ab"], 0.0))
    for un in units:
        un["av"] = _dot(jnp.concatenate([un["a_ak"], un["a_rk"]], axis=0), un["v_s"])
    for un in units:
        un["sv"] = _dot_tn(un["k_t"], un["v_s"])
    for lvl in levels:
        for un in units:
            un["tmp"] = _dot(jnp.where(lvl, un["a_ab"], 0.0), un["inv"])
        for un in units:
            un["inv"] = un["inv"] + _dot(un["inv"], un.pop("tmp"))
    for un in units:
        wu = _dot(un["inv"], jnp.concatenate([un["a_t"], un["av"][:2 * C]], axis=1))
        un["w"], un["u0"], un["y0"] = wu[:, :LANES], wu[:, LANES:], un["av"][2 * C:]

    chains = [(bi, q) for bi in range(nbatch) for q in range(npair)]
    states = [s_ref[bi * npair + q] for bi, q in chains]
    for c in range(nchunk):
        cur = [un for un in units if un["c"] == c]
        us = [un["u0"] + _dot(un["w"], st) for un, st in zip(cur, states)]
        ys = [un["y0"] + _dot(jnp.concatenate([un["r_t"], un["a_rb"]], axis=1),
                              jnp.concatenate([st, u], axis=0))
              for un, st, u in zip(cur, states, us)]
        states = [st * un["g_col"] + un["sv"] + _dot_tn(un["b_t"], u)
                  for un, st, u in zip(cur, states, us)]
        for un, y_s in zip(cur, ys):
            y = y_s[:C] + y_s[C:]
            mean = _head_sum(y, lane_lo) * (1.0 / N)
            yc = y - mean
            var = _head_sum(yc * yc, lane_lo) * (1.0 / N)
            yn = yc * lax.rsqrt(var + GN_EPS) * un["gn_w"] + un["gn_b"]
            o_ref[un["bi"], un["sl"], un["ln"]] = ((yn + un["bonus"]) * un["g"]).astype(o_ref.dtype)
    for idx, (bi, q) in enumerate(chains):
        s_ref[bi * npair + q] = states[idx]


def _rwkv(z_r, pvec, mu_lora, wdu, wau, wgu):
    batch, seq, _ = z_r.shape
    width = wdu.shape[1]
    T = RWKV_STEP
    npair = next(n for n in (RWKV_PAIRS, 2, 1) if (width // LANES) % n == 0)
    lw = npair * LANES
    ng = width // lw
    gate_w = wgu.shape[0]
    lora_w = 2 * LORA_PAD + gate_w
    lora0 = 3 * width // LORA_PAD
    return pl.pallas_call(
        _rwkv_kernel,
        grid=(ng, seq // T),
        in_specs=[
            pl.BlockSpec((batch, T, lw), lambda h, t: (0, t, h)),
            pl.BlockSpec((batch, T, lw), lambda h, t: (0, t, ng + h)),
            pl.BlockSpec((batch, T, lw), lambda h, t: (0, t, 2 * ng + h)),
            pl.BlockSpec((batch, T, LORA_PAD), lambda h, t: (0, t, lora0)),
            pl.BlockSpec((batch, T, LORA_PAD), lambda h, t: (0, t, lora0 + 1)),
            pl.BlockSpec((batch, T, gate_w), lambda h, t: (0, t, (lora0 + 2) * LORA_PAD // gate_w)),
            pl.BlockSpec((16, lw), lambda h, t: (0, h)),
            pl.BlockSpec((1, lora_w), lambda h, t: (0, 0)),
            pl.BlockSpec((LORA_PAD, lw), lambda h, t: (0, h)),
            pl.BlockSpec((LORA_PAD, lw), lambda h, t: (0, h)),
            pl.BlockSpec((gate_w, lw), lambda h, t: (0, h)),
        ],
        out_specs=pl.BlockSpec((batch, T, lw), lambda h, t: (0, t, h)),
        out_shape=jax.ShapeDtypeStruct((batch, seq, width), BF16),
        scratch_shapes=[pltpu.VMEM((batch * npair, LANES, LANES), F32),
                        pltpu.VMEM((batch, SUBLANES, lw), F32),
                        pltpu.VMEM((batch, SUBLANES, lora_w), F32)],
        compiler_params=_cparams(("arbitrary", "arbitrary")),
        name="rwkv7_mix",
    )(z_r, z_r, z_r, z_r, z_r, z_r, pvec, mu_lora, wdu, wau, wgu)


def _split3(c):
    hi = c.astype(BF16).astype(F32)
    mid = (c - hi).astype(BF16).astype(F32)
    lo = (c - hi - mid).astype(BF16).astype(F32)
    return hi, mid, lo


def _moba_kernel(jj_ref, tt_ref, q_ref, k_ref, v_ref, slope_ref, o_ref, ka_ref, vt_ref, qa_ref,
                 kmean_ref, ma_ref, mb_ref, acca_ref, accb_ref, s_ref, *, nb, tile, ntrips):
    BLK, DH = MOBA_BLOCK, MOBA_HEAD_DIM
    nbp = kmean_ref.shape[0]
    neg = NEG_INF * LOG2E

    def rows(n):
        return pl.ds(pl.multiple_of(n * BLK, BLK), BLK)

    kmean_ref[...] = jnp.zeros_like(kmean_ref)
    lane = lax.broadcasted_iota(jnp.int32, (BLK, DH), 1)
    key_off = lax.broadcasted_iota(jnp.int32, (BLK, DH), 0).astype(F32)

    def prep(n, carry):
        kf = k_ref[rows(n), :]
        ext = jnp.where(lane < nbp, jnp.where(lane == n, 1.0, 0.0),
                        jnp.where(lane < nbp + 3, n.astype(F32),
                                  jnp.where(lane < nbp + 6, key_off, 0.0)))
        ka_ref[n] = jnp.concatenate([kf, ext], axis=1).astype(BF16)
        vt_ref[n] = jnp.concatenate([v_ref[rows(n), :].T, jnp.ones((MOBA_ONES, BLK), F32)],
                                    axis=0).astype(BF16)
        kmean_ref[pl.ds(n, 1), :] = jnp.mean(kf, axis=0, keepdims=True)
        return carry

    lax.fori_loop(0, nb, prep, 0)

    slope2 = slope_ref[...] * LOG2E
    coef = _split3(slope2 * float(BLK)) + _split3(slope2)
    erow = lax.broadcasted_iota(jnp.int32, (DH - nbp, BLK), 0)
    ext_q = jnp.zeros((DH - nbp, BLK), F32)
    for idx, cf in enumerate(coef):
        ext_q = jnp.where(erow == idx, cf, ext_q)
    blk_id = lax.broadcasted_iota(jnp.int32, (nbp, BLK), 0)
    kmean = kmean_ref[...]
    causal = (lax.broadcasted_iota(jnp.int32, (BLK, BLK), 1)
              >= lax.broadcasted_iota(jnp.int32, (BLK, BLK), 0))
    lane2 = lax.broadcasted_iota(jnp.int32, (BLK, 2 * DH), 1)
    not_onehot = (lane2 < DH) | (lane2 >= DH + nbp)

    def select(gate, i):
        past = blk_id < i
        gm = jnp.where(past, gate, NEG_INF)
        chosen = jnp.zeros((nbp, BLK), dtype=jnp.bool_)
        for _ in range(MOBA_TOPK):
            mx = jnp.max(gm, axis=0, keepdims=True)
            first = jnp.min(jnp.where(gm == mx, blk_id, nbp), axis=0, keepdims=True)
            pick = blk_id == first
            chosen = chosen | pick
            gm = jnp.where(pick, -jnp.inf, gm)
        return jnp.where(chosen & past, 0.0, neg)

    def qprep(it, carry):
        blocks = [it * tile + r for r in range(tile)]
        q_ts = [q_ref[rows(i), :].T for i in blocks]
        gates = [jnp.dot(kmean, q_t, preferred_element_type=F32,
                         precision=lax.Precision.HIGHEST) for q_t in q_ts]
        selbs = [select(g, i) for g, i in zip(gates, blocks)]
        qas = [jnp.concatenate([q_t * (DH ** -0.5 * LOG2E), sb, ext_q], axis=0).astype(BF16)
               for q_t, sb in zip(q_ts, selbs)]
        for i, qa in zip(blocks, qas):
            qa_ref[i] = qa
        ss = [jnp.where(causal, _dot(jnp.where(not_onehot, ka_ref[i], 0), qa), neg)
              for i, qa in zip(blocks, qas)]
        ms = [jnp.max(s, axis=0, keepdims=True) for s in ss]
        ps = [jnp.exp2(s - m) for s, m in zip(ss, ms)]
        for i, m, p in zip(blocks, ms, ps):
            ma_ref[i] = m
            acca_ref[i] = _dot(vt_ref[i], p.astype(BF16))
            mb_ref[i] = jnp.full((1, BLK), -jnp.inf, F32)
            accb_ref[i] = jnp.zeros(accb_ref.shape[1:], F32)
        return carry

    lax.fori_loop(0, nb // tile, qprep, 0)

    def logits_to(slot, tr):
        kaj = ka_ref[jj_ref[tr]]
        t0 = tt_ref[tr] * tile
        mxs = []
        for r in range(tile):
            s = _dot(kaj, qa_ref[t0 + r])
            s_ref[slot, r] = s
            mxs.append(jnp.max(s, axis=0, keepdims=True))
        return tuple(mxs)

    def half_trip(slot, tr, mxs, tr_next):
        m_ref, acc_ref = (ma_ref, acca_ref) if slot == 0 else (mb_ref, accb_ref)
        mxs_next = logits_to(1 - slot, tr_next)
        vtj = vt_ref[jj_ref[tr]]
        t0 = tt_ref[tr] * tile
        for r in range(tile):
            i = t0 + r
            m = m_ref[i]
            m_new = jnp.maximum(m, mxs[r])
            p = jnp.exp2(s_ref[slot, r] - m_new)
            m_ref[i] = m_new
            acc_ref[i] = acc_ref[i] * jnp.exp2(m - m_new) + _dot(vtj, p.astype(BF16))
        return mxs_next

    def body(pi, mxs):
        tr = MOBA_UNROLL * pi
        for u in range(MOBA_UNROLL - 1):
            mxs = half_trip(u % 2, tr + u, mxs, tr + u + 1)
        last = tr + MOBA_UNROLL - 1
        return half_trip((MOBA_UNROLL - 1) % 2, last, mxs, jnp.minimum(last + 1, ntrips - 1))

    if ntrips:
        lax.fori_loop(0, ntrips // MOBA_UNROLL, body, logits_to(0, 0))

    def finish(it, carry):
        blocks = [it * tile + r for r in range(tile)]
        outs = []
        for i in blocks:
            ma, mb = ma_ref[i], mb_ref[i]
            m = jnp.maximum(ma, mb)
            acc = acca_ref[i] * jnp.exp2(ma - m) + accb_ref[i] * jnp.exp2(mb - m)
            outs.append((acc[:DH] / acc[DH:DH + 1]).T)
        for i, o in zip(blocks, outs):
            o_ref[rows(i), :] = o.astype(o_ref.dtype)
        return carry

    lax.fori_loop(0, nb // tile, finish, 0)


def _moba_schedule(nb, tile):
    trips = [(j, t) for j in range(nb - 1) for t in range((j + 1) // tile, nb // tile)]
    while len(trips) % MOBA_UNROLL:
        trips.append((nb - 1, nb // tile - 1))
    return trips


def _moba(z_m, slopes, *, batch, seq):
    m, cols = z_m.shape
    heads = cols // (3 * MOBA_HEAD_DIM)
    BLK, DH = MOBA_BLOCK, MOBA_HEAD_DIM
    nb = seq // BLK
    nbp = -(-nb // SUBLANES) * SUBLANES
    assert nbp + 6 <= DH
    tile = next(g for g in (MOBA_GROUP, 2, 1) if nb % g == 0)
    trips = _moba_schedule(nb, tile)
    sched = jnp.asarray(trips or [(0, 0)], jnp.int32)
    grid_spec = pltpu.PrefetchScalarGridSpec(
        num_scalar_prefetch=2,
        grid=(batch, heads),
        in_specs=[
            pl.BlockSpec((seq, DH), lambda b, h, jj, tt: (b, h)),
            pl.BlockSpec((seq, DH), lambda b, h, jj, tt: (b, heads + h)),
            pl.BlockSpec((seq, DH), lambda b, h, jj, tt: (b, 2 * heads + h)),
            pl.BlockSpec((None, 1, BLK), lambda b, h, jj, tt: (h, 0, 0)),
        ],
        out_specs=pl.BlockSpec((seq, DH), lambda b, h, jj, tt: (b, h)),
        scratch_shapes=[pltpu.VMEM((nb, BLK, 2 * DH), BF16),
                        pltpu.VMEM((nb, DH + MOBA_ONES, BLK), BF16),
                        pltpu.VMEM((nb, 2 * DH, BLK), BF16),
                        pltpu.VMEM((nbp, DH), F32),
                        pltpu.VMEM((nb, 1, BLK), F32),
                        pltpu.VMEM((nb, 1, BLK), F32),
                        pltpu.VMEM((nb, DH + MOBA_ONES, BLK), F32),
                        pltpu.VMEM((nb, DH + MOBA_ONES, BLK), F32),
                        pltpu.VMEM((2, tile, BLK, BLK), F32)])
    return pl.pallas_call(
        functools.partial(_moba_kernel, nb=nb, tile=tile, ntrips=len(trips)),
        grid_spec=grid_spec,
        out_shape=jax.ShapeDtypeStruct((m, heads * DH), BF16),
        compiler_params=_cparams(("arbitrary", "arbitrary")),
        name="moba_attn",
    )(sched[:, 0], sched[:, 1], z_m, z_m, z_m, slopes)


def _merge_kernel(yr_ref, ym_ref, g1_ref, g2_ref, wr_ref, wm_ref, o_ref):
    br = _dot(yr_ref[...], wr_ref[...])
    bm = _dot(ym_ref[...], wm_ref[...])
    o_ref[...] = (g1_ref[...].astype(F32) * br + g2_ref[...].astype(F32) * bm).astype(o_ref.dtype)


def _merge(y_r, y_m, gates, w_r, w_m, *, tm):
    m, kr = y_r.shape
    km = y_m.shape[1]
    nj, _, tn = w_r.shape
    d = nj * tn
    return pl.pallas_call(
        _merge_kernel,
        grid=(m // tm, nj),
        in_specs=[pl.BlockSpec((tm, kr), lambda i, j: (i, 0)),
                  pl.BlockSpec((tm, km), lambda i, j: (i, 0)),
                  pl.BlockSpec((tm, tn), lambda i, j: (i, j)),
                  pl.BlockSpec((tm, tn), lambda i, j: (i, nj + j)),
                  pl.BlockSpec((None, kr, tn), lambda i, j: (j, 0, 0)),
                  pl.BlockSpec((None, km, tn), lambda i, j: (j, 0, 0))],
        out_specs=pl.BlockSpec((tm, tn), lambda i, j: (i, j)),
        out_shape=jax.ShapeDtypeStruct((m, d), BF16),
        compiler_params=_cparams(("arbitrary", "arbitrary")),
        name="branch_merge",
    )(y_r, y_m, gates, gates, w_r, w_m)


def _resid_mm_kernel(x_ref, a_ref, w_ref, o_ref):
    o_ref[...] = x_ref[...] + _dot(a_ref[...], w_ref[...])


def _resid_mm(x2, a, w, *, tm):
    m, d = x2.shape
    k = a.shape[1]
    nj, _, tn = w.shape
    return pl.pallas_call(
        _resid_mm_kernel,
        grid=(m // tm, nj),
        in_specs=[pl.BlockSpec((tm, tn), lambda i, j: (i, j)),
                  pl.BlockSpec((tm, k), lambda i, j: (i, 0)),
                  pl.BlockSpec((None, k, tn), lambda i, j: (j, 0, 0))],
        out_specs=pl.BlockSpec((tm, tn), lambda i, j: (i, j)),
        out_shape=jax.ShapeDtypeStruct((m, d), F32),
        compiler_params=_cparams(("arbitrary", "arbitrary")),
        name="out_proj",
    )(x2, a, w)


def _mlp_kernel(x_ref, g_ref, w1_ref, w2_ref, o_ref, h_ref):
    @pl.when(pl.program_id(1) == 0)
    def _():
        xf = x_ref[...]
        h_ref[...] = _rms(xf, g_ref[...]).astype(BF16)
        o_ref[...] = xf

    u = jnp.maximum(_dot(h_ref[...], w1_ref[...]), 0.0)
    o_ref[...] += _dot((u * u).astype(BF16), w2_ref[...])


def _mlp(x2, g, w1, w2, *, tm):
    m, d = x2.shape
    nf, _, tf = w1.shape
    return pl.pallas_call(
        _mlp_kernel,
        grid=(m // tm, nf),
        in_specs=[pl.BlockSpec((tm, d), lambda i, j: (i, 0)),
                  pl.BlockSpec((1, d), lambda i, j: (0, 0)),
                  pl.BlockSpec((None, d, tf), lambda i, j: (j, 0, 0)),
                  pl.BlockSpec((tf, d), lambda i, j: (j, 0))],
        out_specs=pl.BlockSpec((tm, d), lambda i, j: (i, 0)),
        out_shape=jax.ShapeDtypeStruct((m, d), F32),
        scratch_shapes=[pltpu.VMEM((tm, d), BF16)],
        compiler_params=_cparams(("arbitrary", "arbitrary")),
        name="mlp_relu2",
    )(x2, g, w1, w2)


def _ple_kernel(x_ref, p_ref, g_ref, gf_ref, wg_ref, wu_ref, o_ref, *, final_norm):
    xf = x_ref[...]
    hp = _rms(xf, g_ref[...]).astype(BF16)
    gate = jax.nn.sigmoid(_dot(hp, wg_ref[...]))
    up = _dot(p_ref[...].astype(BF16), wu_ref[...])
    y = xf + gate * up
    o_ref[...] = _rms(y, gf_ref[...]) if final_norm else y


def _ple(x2, p2, g, gf, wg, wu, *, tm, final_norm):
    m, d = x2.shape
    pd = p2.shape[1]
    return pl.pallas_call(
        functools.partial(_ple_kernel, final_norm=final_norm),
        grid=(m // tm,),
        in_specs=[pl.BlockSpec((tm, d), lambda i: (i, 0)),
                  pl.BlockSpec((tm, pd), lambda i: (i, 0)),
                  pl.BlockSpec((1, d), lambda i: (0, 0)),
                  pl.BlockSpec((1, d), lambda i: (0, 0)),
                  pl.BlockSpec((d, d), lambda i: (0, 0)),
                  pl.BlockSpec((pd, d), lambda i: (0, 0))],
        out_specs=pl.BlockSpec((tm, d), lambda i: (i, 0)),
        out_shape=jax.ShapeDtypeStruct((m, d), F32),
        compiler_params=_cparams(("arbitrary",)),
        name="ple_final",
    )(x2, p2, g, gf, wg, wu)


def _tile_plan(m):
    def rows(cap):
        return next(t for t in (1024, 512, 256, 128, 64, 32, 16, 8) if t <= cap and m % t == 0)

    return dict(tn=512, tm_in=rows(1024), tm_out=rows(1024), tm_mlp=rows(1024), tm_ple=rows(256))


def _pad_to(a, n, axis):
    pad = [(0, 0)] * a.ndim
    pad[axis] = (0, n - a.shape[axis])
    return jnp.pad(a, pad)


def kernel(x, p, g_mix, w_in, mu_shift, w0, w_decay_up, a0, w_aaa_up, w_gate_up, k_k, k_a, r_k,
           gn_w, gn_b, w_branch_rwkv, w_branch_moba, w_out, g_mlp, w_mlp_in, w_mlp_out, g_ple,
           w_ple_gate, w_ple_up, g_final):
    B, S, D = x.shape
    depth = w_in.shape[0]
    W = w0.shape[1]
    Wm = w_branch_moba.shape[1]
    dl, al, gl = w_decay_up.shape[1], w_aaa_up.shape[1], w_gate_up.shape[1]
    assert S % MOBA_BLOCK == 0 and S % RWKV_STEP == 0
    assert W % LANES == 0 and dl <= LORA_PAD and al <= LORA_PAD and gl % LANES == 0
    M = B * S
    tiles = _tile_plan(M)
    x2 = x.reshape(M, D)
    moba_heads = Wm // MOBA_HEAD_DIM
    slopes = 2.0 ** (-8.0 * (jnp.arange(moba_heads, dtype=F32) + 1.0) / moba_heads)
    slopes = jnp.broadcast_to(slopes[:, None, None], (moba_heads, 1, MOBA_BLOCK))

    for i in range(depth):
        wi = w_in[i]
        c0 = 3 * W
        c1, c2, c3 = c0 + dl, c0 + dl + al, c0 + dl + al + gl
        tn = tiles["tn"]
        w_r = _col_tiles(jnp.concatenate([wi[:, :c0], _pad_to(wi[:, c0:c1], LORA_PAD, 1),
                                          _pad_to(wi[:, c1:c2], LORA_PAD, 1), wi[:, c2:c3]],
                                         axis=1), tn)
        w_m = _col_tiles(wi[:, c3:c3 + 3 * Wm], tn)
        w_g = _col_tiles(wi[:, c3 + 3 * Wm:], tn)
        mu = mu_shift[i]
        mu_lora = jnp.concatenate([_pad_to(mu[c0:c1], LORA_PAD, 0), _pad_to(mu[c1:c2], LORA_PAD, 0),
                                   mu[c2:c3]])[None, :]
        pvec = jnp.stack([w0[i], a0[i], k_k[i], k_a[i], r_k[i].reshape(-1), gn_w[i], gn_b[i],
                          mu[:W], mu[W:2 * W], mu[2 * W:3 * W]])
        pvec = _pad_to(pvec, 16, 0)
        wdu = _pad_to(w_decay_up[i], LORA_PAD, 0)
        wau = _pad_to(w_aaa_up[i], LORA_PAD, 0)
        gmix = g_mix[i][None, :]

        h = _norm(x2, gmix, tm=tiles["tm_ple"])
        z_r = _inproj(h, w_r, act="none", out_dtype=F32, tm=tiles["tm_in"])
        z_m = _inproj(h, w_m, act="none", out_dtype=F32, tm=tiles["tm_in"])
        gates = _inproj(h, w_g, act="sigmoid", out_dtype=BF16, tm=tiles["tm_in"])

        y_r = _rwkv(z_r.reshape(B, S, -1), pvec, mu_lora, wdu, wau, w_gate_up[i]).reshape(M, W)
        y_m = _moba(z_m, slopes, batch=B, seq=S)

        merged = _merge(y_r, y_m, gates, _col_tiles(w_branch_rwkv[i], tn),
                        _col_tiles(w_branch_moba[i], tn), tm=tiles["tm_out"])
        x2 = _resid_mm(x2, merged, _col_tiles(w_out[i], tn), tm=tiles["tm_out"])
        x2 = _mlp(x2, g_mlp[i][None, :], _col_tiles(w_mlp_in[i], tn), w_mlp_out[i].astype(BF16),
                  tm=tiles["tm_mlp"])
        x2 = _ple(x2, p[i].reshape(M, -1), g_ple[i][None, :], g_final[None, :],
                  w_ple_gate[i].astype(BF16), w_ple_up[i].astype(BF16), tm=tiles["tm_ple"],
                  final_norm=(i == depth - 1))
    return x2.reshape(B, S, D)
```

```python
import functools
import math

import jax
import jax.numpy as jnp
from jax import lax
from jax.experimental import pallas as pl
from jax.experimental.pallas import tpu as pltpu

F32 = jnp.float32
BF16 = jnp.bfloat16

NORM_EPS = 1e-6
GN_EPS = 64e-5
NEG_INF = -1e30
LOG2E = math.log2(math.e)

LANES = 128
SUBLANES = 8
RWKV_HEAD_DIM = 64
RWKV_CHUNK = 64
RWKV_STEP = 128
RWKV_PAIRS = 4
MOBA_HEAD_DIM = 128
MOBA_BLOCK = 256
MOBA_TOPK = 3
MOBA_GROUP = 4
MOBA_ONES = 16
MOBA_UNROLL = 8
LORA_PAD = 128
VMEM_LIMIT = 56 * 1024 * 1024


def _cparams(sem):
    return pltpu.CompilerParams(dimension_semantics=sem, vmem_limit_bytes=VMEM_LIMIT)


def _rms(xf, g):
    return xf * lax.rsqrt(jnp.mean(xf * xf, axis=-1, keepdims=True) + NORM_EPS) * g


def _dot(a, b):
    return jnp.dot(a.astype(BF16), b.astype(BF16), preferred_element_type=F32)


def _split3(c):
    hi = c.astype(BF16).astype(F32)
    mid = (c - hi).astype(BF16).astype(F32)
    lo = (c - hi - mid).astype(BF16).astype(F32)
    return hi, mid, lo


def _dot_nt(a, b):
    return lax.dot_general(a.astype(BF16), b.astype(BF16), (((1,), (1,)), ((), ())),
                           preferred_element_type=F32)


def _dot_tn(a, b):
    return lax.dot_general(a.astype(BF16), b.astype(BF16), (((0,), (0,)), ((), ())),
                           preferred_element_type=F32)


def _norm_kernel(x_ref, g_ref, o_ref):
    o_ref[...] = _rms(x_ref[...], g_ref[...]).astype(o_ref.dtype)


def _norm(x2, g, *, tm):
    m, d = x2.shape
    return pl.pallas_call(
        _norm_kernel,
        grid=(m // tm,),
        in_specs=[pl.BlockSpec((tm, d), lambda i: (i, 0)),
                  pl.BlockSpec((1, d), lambda i: (0, 0))],
        out_specs=pl.BlockSpec((tm, d), lambda i: (i, 0)),
        out_shape=jax.ShapeDtypeStruct((m, d), BF16),
        compiler_params=_cparams(("arbitrary",)),
        name="mix_norm",
    )(x2, g)


def _inproj_kernel(h_ref, w_ref, o_ref, *, act):
    z = _dot(h_ref[...], w_ref[...])
    if act == "sigmoid":
        z = jax.nn.sigmoid(z)
    o_ref[...] = z.astype(o_ref.dtype)


def _col_tiles(w, tn):
    k, n = w.shape
    return w.astype(BF16).reshape(k, n // tn, tn).transpose(1, 0, 2)


def _inproj(h, w, *, act, out_dtype, tm):
    m, d = h.shape
    nt, _, tn = w.shape
    n = nt * tn
    return pl.pallas_call(
        functools.partial(_inproj_kernel, act=act),
        grid=(m // tm, nt),
        in_specs=[pl.BlockSpec((tm, d), lambda i, j: (i, 0)),
                  pl.BlockSpec((None, d, tn), lambda i, j: (j, 0, 0))],
        out_specs=pl.BlockSpec((tm, tn), lambda i, j: (i, j)),
        out_shape=jax.ShapeDtypeStruct((m, n), out_dtype),
        compiler_params=_cparams(("arbitrary", "arbitrary")),
        name="inproj_" + act,
    )(h, w)


def _head_sum(x, lane_lo):
    s0 = jnp.sum(jnp.where(lane_lo, x, 0.0), axis=-1, keepdims=True)
    s1 = jnp.sum(jnp.where(lane_lo, 0.0, x), axis=-1, keepdims=True)
    return jnp.where(lane_lo, s0, s1)


def _rwkv_kernel(r_ref, k_ref, v_ref, wd_ref, ad_ref, gd_ref, pv_ref, mul_ref,
                 wdu_ref, wau_ref, wgu_ref, o_ref, s_ref, prev_ref, prevl_ref):
    C, T, N = RWKV_CHUNK, RWKV_STEP, RWKV_HEAD_DIM
    nbatch = r_ref.shape[0]
    width = r_ref.shape[2]
    npair = width // LANES
    nchunk = T // C

    @pl.when(pl.program_id(1) == 0)
    def _():
        s_ref[...] = jnp.zeros_like(s_ref)
        prev_ref[...] = jnp.zeros_like(prev_ref)
        prevl_ref[...] = jnp.zeros_like(prevl_ref)

    row_t = lax.broadcasted_iota(jnp.int32, (T, 1), 0)

    def shift(z, prev_row, mu):
        zp = jnp.where(row_t == 0, prev_row, pltpu.roll(z, 1, 0))
        return z + mu * (zp - z)

    pv = pv_ref[...]
    w0, a0, k_k, k_a, r_k, gn_w, gn_b = (pv[i:i + 1] for i in range(7))
    mu_r, mu_k, mu_v = pv[7:8], pv[8:9], pv[9:10]
    mul = mul_ref[...]
    lane_lo = lax.broadcasted_iota(jnp.int32, (1, LANES), 1) < N

    def head_sum(x):
        return jnp.concatenate([_head_sum(x[:, q * LANES:(q + 1) * LANES], lane_lo)
                                for q in range(npair)], axis=1)

    row = lax.broadcasted_iota(jnp.int32, (2 * C, 2 * C), 0)
    col = lax.broadcasted_iota(jnp.int32, (2 * C, 2 * C), 1)
    same_head = (row // C) == (col // C)
    strict = same_head & (row > col)
    incl = same_head & (row >= col)
    eye = row == col
    pair = (row // 2) == (col // 2)
    levels = []
    b = 2
    while b < C:
        levels.append(((row // (2 * b)) == (col // (2 * b))) & ((row // b) != (col // b)))
        b *= 2
    tri = (lax.broadcasted_iota(jnp.int32, (C, C), 0)
           >= lax.broadcasted_iota(jnp.int32, (C, C), 1)).astype(F32)

    def stack(x):
        return jnp.concatenate([jnp.where(lane_lo, x, 0.0), jnp.where(lane_lo, 0.0, x)], axis=0)

    units = []
    for bi in range(nbatch):
        zr, zk, zv = r_ref[bi], k_ref[bi], v_ref[bi]
        zwd, zad, zgd = wd_ref[bi], ad_ref[bi], gd_ref[bi]
        pr = prev_ref[bi]
        pl_ = prevl_ref[bi]
        r = shift(zr, pr[0:1], mu_r)
        k = shift(zk, pr[1:2], mu_k)
        v = shift(zv, pr[2:3], mu_v)
        wd = shift(zwd, pl_[0:1, 0:LORA_PAD], mul[:, 0:LORA_PAD])
        ad = shift(zad, pl_[0:1, LORA_PAD:2 * LORA_PAD], mul[:, LORA_PAD:2 * LORA_PAD])
        gd = shift(zgd, pl_[0:1, 2 * LORA_PAD:], mul[:, 2 * LORA_PAD:])
        prev_ref[bi, 0:1, :] = zr[T - 1:T]
        prev_ref[bi, 1:2, :] = zk[T - 1:T]
        prev_ref[bi, 2:3, :] = zv[T - 1:T]
        prevl_ref[bi, 0:1, 0:LORA_PAD] = zwd[T - 1:T]
        prevl_ref[bi, 0:1, LORA_PAD:2 * LORA_PAD] = zad[T - 1:T]
        prevl_ref[bi, 0:1, 2 * LORA_PAD:] = zgd[T - 1:T]

        w_log = -jax.nn.softplus(-(w0 + _dot(jnp.tanh(wd), wdu_ref[...]))) - 0.5
        lw = -jnp.exp(w_log)
        a = jax.nn.sigmoid(a0 + _dot(ad, wau_ref[...]))
        g = _dot(jax.nn.sigmoid(gd), wgu_ref[...])
        kk = k * k_k
        kk = kk / jnp.maximum(jnp.sqrt(head_sum(kk * kk)), 1e-12)
        kmod = k * (1.0 + (a - 1.0) * k_a)
        beta = kk * a
        bonus = head_sum(r * kmod * r_k) * v

        cums = []
        for c in range(nchunk):
            parts = _dot(tri, jnp.concatenate(_split3(lw[c * C:(c + 1) * C]), axis=1))
            cums.append(parts[:, :width] + parts[:, width:2 * width] + parts[:, 2 * width:])

        for q in range(npair):
            ln = slice(q * LANES, (q + 1) * LANES)
            for c in range(nchunk):
                sl = slice(c * C, (c + 1) * C)
                units.append(dict(bi=bi, q=q, c=c, sl=sl, ln=ln, lw=lw[sl, ln], cum=cums[c][:, ln],
                                  kk=kk[sl, ln], r=r[sl, ln], beta=beta[sl, ln],
                                  kmod=kmod[sl, ln], v=v[sl, ln], bonus=bonus[sl, ln],
                                  g=g[sl, ln], gn_w=gn_w[:, ln], gn_b=gn_b[:, ln]))

    for un in units:
        cum, lw_c = un["cum"], un["lw"]
        cum_end = cum[C - 1:C]
        e_inv = jnp.exp(-cum)
        e_rem = jnp.exp(cum_end - cum)
        un["a_t"] = stack(-un["kk"] * jnp.exp(cum - lw_c))
        un["r_t"] = stack(un["r"] * jnp.exp(cum))
        un["b_h"] = un["beta"] * e_inv
        un["k_h"] = un["kmod"] * e_inv
        un["b_t"] = stack(un["beta"] * e_rem)
        un["k_t"] = stack(un["kmod"] * e_rem)
        un["v_s"] = stack(un["v"])
        un["g_col"] = jnp.sum(jnp.where(eye, jnp.exp(cum_end), 0.0), axis=1, keepdims=True)
    for un in units:
        un["scores"] = _dot_nt(jnp.concatenate([un["a_t"], un["r_t"]], axis=0),
                               jnp.concatenate([un["b_h"], un["b_h"], un["k_h"], un["k_h"]],
                                               axis=0))
    for un in units:
        sc = un.pop("scores")
        un["a_ab"] = jnp.where(strict, sc[:2 * C, :2 * C], 0.0)
        un["a_ak"] = jnp.where(strict, sc[:2 * C, 2 * C:], 0.0)
        un["a_rb"] = jnp.where(incl, sc[2 * C:, :2 * C], 0.0)
        un["a_rk"] = jnp.where(incl, sc[2 * C:, 2 * C:], 0.0)
        un["inv"] = jnp.where(eye, 1.0, jnp.where(pair, un["a_ab"], 0.0))
    for un in units:
        un["av"] = _dot(jnp.concatenate([un["a_ak"], un["a_rk"]], axis=0), un["v_s"])
    for un in units:
        un["sv"] = _dot_tn(un["k_t"], un["v_s"])
    for lvl in levels:
        for un in units:
            un["tmp"] = _dot(jnp.where(lvl, un["a_ab"], 0.0), un["inv"])
        for un in units:
            un["inv"] = un["inv"] + _dot(un["inv"], un.pop("tmp"))
    for un in units:
        wu = _dot(un["inv"], jnp.concatenate([un["a_t"], un["av"][:2 * C]], axis=1))
        un["w"], un["u0"], un["y0"] = wu[:, :LANES], wu[:, LANES:], un["av"][2 * C:]

    chains = [(bi, q) for bi in range(nbatch) for q in range(npair)]
    states = [s_ref[bi * npair + q] for bi, q in chains]
    for c in range(nchunk):
        cur = [un for un in units if un["c"] == c]
        us = [un["u0"] + _dot(un["w"], st) for un, st in zip(cur, states)]
        ys = [un["y0"] + _dot(jnp.concatenate([un["r_t"], un["a_rb"]], axis=1),
                              jnp.concatenate([st, u], axis=0))
              for un, st, u in zip(cur, states, us)]
        states = [st * un["g_col"] + un["sv"] + _dot_tn(un["b_t"], u)
                  for un, st, u in zip(cur, states, us)]
        for un, y_s in zip(cur, ys):
            y = y_s[:C] + y_s[C:]
            mean = _head_sum(y, lane_lo) * (1.0 / N)
            yc = y - mean
            var = _head_sum(yc * yc, lane_lo) * (1.0 / N)
            yn = yc * lax.rsqrt(var + GN_EPS) * un["gn_w"] + un["gn_b"]
            o_ref[un["bi"], un["sl"], un["ln"]] = ((yn + un["bonus"]) * un["g"]).astype(o_ref.dtype)
    for idx, (bi, q) in enumerate(chains):
        s_ref[bi * npair + q] = states[idx]


def _rwkv(z_r, pvec, mu_lora, wdu, wau, wgu):
    batch, seq, _ = z_r.shape
    width = wdu.shape[1]
    T = RWKV_STEP
    npair = next(n for n in (RWKV_PAIRS, 2, 1) if (width // LANES) % n == 0)
    lw = npair * LANES
    ng = width // lw
    gate_w = wgu.shape[0]
    lora_w = 2 * LORA_PAD + gate_w
    lora0 = 3 * width // LORA_PAD
    return pl.pallas_call(
        _rwkv_kernel,
        grid=(ng, seq // T),
        in_specs=[
            pl.BlockSpec((batch, T, lw), lambda h, t: (0, t, h)),
            pl.BlockSpec((batch, T, lw), lambda h, t: (0, t, ng + h)),
            pl.BlockSpec((batch, T, lw), lambda h, t: (0, t, 2 * ng + h)),
            pl.BlockSpec((batch, T, LORA_PAD), lambda h, t: (0, t, lora0)),
            pl.BlockSpec((batch, T, LORA_PAD), lambda h, t: (0, t, lora0 + 1)),
            pl.BlockSpec((batch, T, gate_w), lambda h, t: (0, t, (lora0 + 2) * LORA_PAD // gate_w)),
            pl.BlockSpec((16, lw), lambda h, t: (0, h)),
            pl.BlockSpec((1, lora_w), lambda h, t: (0, 0)),
            pl.BlockSpec((LORA_PAD, lw), lambda h, t: (0, h)),
            pl.BlockSpec((LORA_PAD, lw), lambda h, t: (0, h)),
            pl.BlockSpec((gate_w, lw), lambda h, t: (0, h)),
        ],
        out_specs=pl.BlockSpec((batch, T, lw), lambda h, t: (0, t, h)),
        out_shape=jax.ShapeDtypeStruct((batch, seq, width), BF16),
        scratch_shapes=[pltpu.VMEM((batch * npair, LANES, LANES), F32),
                        pltpu.VMEM((batch, SUBLANES, lw), F32),
                        pltpu.VMEM((batch, SUBLANES, lora_w), F32)],
        compiler_params=_cparams(("arbitrary", "arbitrary")),
        name="rwkv7_mix",
    )(z_r, z_r, z_r, z_r, z_r, z_r, pvec, mu_lora, wdu, wau, wgu)


def _moba_kernel(jj_ref, tt_ref, q_ref, k_ref, v_ref, slope_ref, o_ref, ka_ref, vt_ref, qa_ref,
                 kmean_ref, ma_ref, mb_ref, acca_ref, accb_ref, s_ref, *, nb, tile, ntrips):
    BLK, DH = MOBA_BLOCK, MOBA_HEAD_DIM
    nbp = kmean_ref.shape[0]
    neg = NEG_INF * LOG2E

    def rows(n):
        return pl.ds(pl.multiple_of(n * BLK, BLK), BLK)

    kmean_ref[...] = jnp.zeros_like(kmean_ref)
    lane = lax.broadcasted_iota(jnp.int32, (BLK, DH), 1)
    key_off = lax.broadcasted_iota(jnp.int32, (BLK, DH), 0).astype(F32)

    def prep(n, carry):
        kf = k_ref[rows(n), :]
        ext = jnp.where(lane < nbp, jnp.where(lane == n, 1.0, 0.0),
                        jnp.where(lane < nbp + 3, n.astype(F32),
                                  jnp.where(lane < nbp + 6, key_off, 0.0)))
        ka_ref[n] = jnp.concatenate([kf, ext], axis=1).astype(BF16)
        vt_ref[n] = jnp.concatenate([v_ref[rows(n), :].T, jnp.ones((MOBA_ONES, BLK), F32)],
                                    axis=0).astype(BF16)
        kmean_ref[pl.ds(n, 1), :] = jnp.mean(kf, axis=0, keepdims=True)
        return carry

    lax.fori_loop(0, nb, prep, 0)

    slope2 = slope_ref[...] * LOG2E
    coef = _split3(slope2 * float(BLK)) + _split3(slope2)
    erow = lax.broadcasted_iota(jnp.int32, (DH - nbp, BLK), 0)
    ext_q = jnp.zeros((DH - nbp, BLK), F32)
    for idx, cf in enumerate(coef):
        ext_q = jnp.where(erow == idx, cf, ext_q)
    blk_id = lax.broadcasted_iota(jnp.int32, (nbp, BLK), 0)
    k1, k2, k3 = _split3(kmean_ref[...])
    kmean6 = jnp.concatenate([k1, k1, k2, k1, k2, k3], axis=1)
    causal = (lax.broadcasted_iota(jnp.int32, (BLK, BLK), 1)
              >= lax.broadcasted_iota(jnp.int32, (BLK, BLK), 0))
    lane2 = lax.broadcasted_iota(jnp.int32, (BLK, 2 * DH), 1)
    not_onehot = (lane2 < DH) | (lane2 >= DH + nbp)

    def select(gate, i):
        past = blk_id < i
        gm = jnp.where(past, gate, NEG_INF)
        chosen = jnp.zeros((nbp, BLK), dtype=jnp.bool_)
        for _ in range(MOBA_TOPK):
            mx = jnp.max(gm, axis=0, keepdims=True)
            first = jnp.min(jnp.where(gm == mx, blk_id, nbp), axis=0, keepdims=True)
            pick = blk_id == first
            chosen = chosen | pick
            gm = jnp.where(pick, -jnp.inf, gm)
        return jnp.where(chosen & past, 0.0, neg)

    def qprep(it, carry):
        blocks = [it * tile + r for r in range(tile)]
        q_ts = [q_ref[rows(i), :].T for i in blocks]
        gates = []
        for q_t in q_ts:
            q1, q2, q3 = _split3(q_t)
            gates.append(_dot(kmean6, jnp.concatenate([q1, q2, q1, q3, q2, q1], axis=0)))
        selbs = [select(g, i) for g, i in zip(gates, blocks)]
        qas = [jnp.concatenate([q_t * (DH ** -0.5 * LOG2E), sb, ext_q], axis=0).astype(BF16)
               for q_t, sb in zip(q_ts, selbs)]
        for i, qa in zip(blocks, qas):
            qa_ref[i] = qa
        ss = [jnp.where(causal, _dot(jnp.where(not_onehot, ka_ref[i], 0), qa), neg)
              for i, qa in zip(blocks, qas)]
        ms = [jnp.max(s, axis=0, keepdims=True) for s in ss]
        ps = [jnp.exp2(s - m) for s, m in zip(ss, ms)]
        for i, m, p in zip(blocks, ms, ps):
            ma_ref[i] = m
            acca_ref[i] = _dot(vt_ref[i], p.astype(BF16))
            mb_ref[i] = jnp.full((1, BLK), -jnp.inf, F32)
            accb_ref[i] = jnp.zeros(accb_ref.shape[1:], F32)
        return carry

    lax.fori_loop(0, nb // tile, qprep, 0)

    def logits_to(slot, tr):
        kaj = ka_ref[jj_ref[tr]]
        t0 = tt_ref[tr] * tile
        mxs = []
        for r in range(tile):
            s = _dot(kaj, qa_ref[t0 + r])
            s_ref[slot, r] = s
            mxs.append(jnp.max(s, axis=0, keepdims=True))
        return tuple(mxs)

    def half_trip(slot, tr, mxs, tr_next):
        m_ref, acc_ref = (ma_ref, acca_ref) if slot == 0 else (mb_ref, accb_ref)
        mxs_next = logits_to(1 - slot, tr_next)
        vtj = vt_ref[jj_ref[tr]]
        t0 = tt_ref[tr] * tile
        for r in range(tile):
            i = t0 + r
            m = m_ref[i]
            m_new = jnp.maximum(m, mxs[r])
            p = jnp.exp2(s_ref[slot, r] - m_new)
            m_ref[i] = m_new
            acc_ref[i] = acc_ref[i] * jnp.exp2(m - m_new) + _dot(vtj, p.astype(BF16))
        return mxs_next

    def body(pi, mxs):
        tr = MOBA_UNROLL * pi
        for u in range(MOBA_UNROLL - 1):
            mxs = half_trip(u % 2, tr + u, mxs, tr + u + 1)
        last = tr + MOBA_UNROLL - 1
        return half_trip((MOBA_UNROLL - 1) % 2, last, mxs, jnp.minimum(last + 1, ntrips - 1))

    if ntrips:
        lax.fori_loop(0, ntrips // MOBA_UNROLL, body, logits_to(0, 0))

    def finish(it, carry):
        blocks = [it * tile + r for r in range(tile)]
        outs = []
        for i in blocks:
            ma, mb = ma_ref[i], mb_ref[i]
            m = jnp.maximum(ma, mb)
            acc = acca_ref[i] * jnp.exp2(ma - m) + accb_ref[i] * jnp.exp2(mb - m)
            outs.append((acc[:DH] / acc[DH:DH + 1]).T)
        for i, o in zip(blocks, outs):
            o_ref[rows(i), :] = o.astype(o_ref.dtype)
        return carry

    lax.fori_loop(0, nb // tile, finish, 0)


def _moba_schedule(nb, tile):
    trips = [(j, t) for j in range(nb - 1) for t in range((j + 1) // tile, nb // tile)]
    while len(trips) % MOBA_UNROLL:
        trips.append((nb - 1, nb // tile - 1))
    return trips


def _moba(z_m, slopes, *, batch, seq):
    m, cols = z_m.shape
    heads = cols // (3 * MOBA_HEAD_DIM)
    BLK, DH = MOBA_BLOCK, MOBA_HEAD_DIM
    nb = seq // BLK
    nbp = -(-nb // SUBLANES) * SUBLANES
    assert nbp + 6 <= DH
    tile = next(g for g in (MOBA_GROUP, 2, 1) if nb % g == 0)
    trips = _moba_schedule(nb, tile)
    sched = jnp.asarray(trips or [(0, 0)], jnp.int32)
    grid_spec = pltpu.PrefetchScalarGridSpec(
        num_scalar_prefetch=2,
        grid=(batch, heads),
        in_specs=[
            pl.BlockSpec((seq, DH), lambda b, h, jj, tt: (b, h)),
            pl.BlockSpec((seq, DH), lambda b, h, jj, tt: (b, heads + h)),
            pl.BlockSpec((seq, DH), lambda b, h, jj, tt: (b, 2 * heads + h)),
            pl.BlockSpec((None, 1, BLK), lambda b, h, jj, tt: (h, 0, 0)),
        ],
        out_specs=pl.BlockSpec((seq, DH), lambda b, h, jj, tt: (b, h)),
        scratch_shapes=[pltpu.VMEM((nb, BLK, 2 * DH), BF16),
                        pltpu.VMEM((nb, DH + MOBA_ONES, BLK), BF16),
                        pltpu.VMEM((nb, 2 * DH, BLK), BF16),
                        pltpu.VMEM((nbp, DH), F32),
                        pltpu.VMEM((nb, 1, BLK), F32),
                        pltpu.VMEM((nb, 1, BLK), F32),
                        pltpu.VMEM((nb, DH + MOBA_ONES, BLK), F32),
                        pltpu.VMEM((nb, DH + MOBA_ONES, BLK), F32),
                        pltpu.VMEM((2, tile, BLK, BLK), F32)])
    return pl.pallas_call(
        functools.partial(_moba_kernel, nb=nb, tile=tile, ntrips=len(trips)),
        grid_spec=grid_spec,
        out_shape=jax.ShapeDtypeStruct((m, heads * DH), BF16),
        compiler_params=_cparams(("arbitrary", "arbitrary")),
        name="moba_attn",
    )(sched[:, 0], sched[:, 1], z_m, z_m, z_m, slopes)


def _resident(shape):
    return pl.BlockSpec(shape, lambda i: (0,) * len(shape), pipeline_mode=pl.Buffered(1))


def _merge_out_kernel(x_ref, yr_ref, ym_ref, g_ref, wr_ref, wm_ref, wo_ref, o_ref):
    d = o_ref.shape[1]
    br = _dot(yr_ref[...], wr_ref[...])
    bm = _dot(ym_ref[...], wm_ref[...])
    merged = g_ref[:, :d].astype(F32) * br + g_ref[:, d:].astype(F32) * bm
    o_ref[...] = x_ref[...] + _dot(merged, wo_ref[...])


def _merge_out(x2, y_r, y_m, gates, w_r, w_m, w_o, *, tm):
    m, d = x2.shape
    kr, km = y_r.shape[1], y_m.shape[1]
    return pl.pallas_call(
        _merge_out_kernel,
        grid=(m // tm,),
        in_specs=[pl.BlockSpec((tm, d), lambda i: (i, 0)),
                  pl.BlockSpec((tm, kr), lambda i: (i, 0)),
                  pl.BlockSpec((tm, km), lambda i: (i, 0)),
                  pl.BlockSpec((tm, 2 * d), lambda i: (i, 0)),
                  _resident((kr, d)), _resident((km, d)), _resident((d, d))],
        out_specs=pl.BlockSpec((tm, d), lambda i: (i, 0)),
        out_shape=jax.ShapeDtypeStruct((m, d), F32),
        compiler_params=_cparams(("arbitrary",)),
        name="merge_out_proj",
    )(x2, y_r, y_m, gates, w_r, w_m, w_o)


def _mlp_kernel(x_ref, g_ref, w1_ref, w2_ref, o_ref, h_ref):
    @pl.when(pl.program_id(1) == 0)
    def _():
        xf = x_ref[...]
        h_ref[...] = _rms(xf, g_ref[...]).astype(BF16)
        o_ref[...] = xf

    u = jnp.maximum(_dot(h_ref[...], w1_ref[...]), 0.0)
    o_ref[...] += _dot((u * u).astype(BF16), w2_ref[...])


def _mlp(x2, g, w1, w2, *, tm):
    m, d = x2.shape
    nf, _, tf = w1.shape
    return pl.pallas_call(
        _mlp_kernel,
        grid=(m // tm, nf),
        in_specs=[pl.BlockSpec((tm, d), lambda i, j: (i, 0)),
                  pl.BlockSpec((1, d), lambda i, j: (0, 0)),
                  pl.BlockSpec((None, d, tf), lambda i, j: (j, 0, 0)),
                  pl.BlockSpec((tf, d), lambda i, j: (j, 0))],
        out_specs=pl.BlockSpec((tm, d), lambda i, j: (i, 0)),
        out_shape=jax.ShapeDtypeStruct((m, d), F32),
        scratch_shapes=[pltpu.VMEM((tm, d), BF16)],
        compiler_params=_cparams(("arbitrary", "arbitrary")),
        name="mlp_relu2",
    )(x2, g, w1, w2)


def _ple_kernel(x_ref, p_ref, g_ref, gf_ref, wg_ref, wu_ref, o_ref, *, final_norm):
    xf = x_ref[...]
    hp = _rms(xf, g_ref[...]).astype(BF16)
    gate = jax.nn.sigmoid(_dot(hp, wg_ref[...]))
    up = _dot(p_ref[...].astype(BF16), wu_ref[...])
    y = xf + gate * up
    o_ref[...] = _rms(y, gf_ref[...]) if final_norm else y


def _ple(x2, p2, g, gf, wg, wu, *, tm, final_norm):
    m, d = x2.shape
    pd = p2.shape[1]
    return pl.pallas_call(
        functools.partial(_ple_kernel, final_norm=final_norm),
        grid=(m // tm,),
        in_specs=[pl.BlockSpec((tm, d), lambda i: (i, 0)),
                  pl.BlockSpec((tm, pd), lambda i: (i, 0)),
                  pl.BlockSpec((1, d), lambda i: (0, 0)),
                  pl.BlockSpec((1, d), lambda i: (0, 0)),
                  _resident((d, d)), _resident((pd, d))],
        out_specs=pl.BlockSpec((tm, d), lambda i: (i, 0)),
        out_shape=jax.ShapeDtypeStruct((m, d), F32),
        compiler_params=_cparams(("arbitrary",)),
        name="ple_final",
    )(x2, p2, g, gf, wg, wu)


def _tile_plan(m):
    def rows(cap):
        return next(t for t in (2048, 1024, 512, 256, 128, 64, 32, 16, 8)
                    if t <= cap and m % t == 0)

    return dict(tn=512, tm_norm=rows(512), tm_in=rows(2048), tm_out=rows(512), tm_mlp=rows(1024),
                tm_ple=rows(512))


def _pad_to(a, n, axis):
    pad = [(0, 0)] * a.ndim
    pad[axis] = (0, n - a.shape[axis])
    return jnp.pad(a, pad)


def kernel(x, p, g_mix, w_in, mu_shift, w0, w_decay_up, a0, w_aaa_up, w_gate_up, k_k, k_a, r_k,
           gn_w, gn_b, w_branch_rwkv, w_branch_moba, w_out, g_mlp, w_mlp_in, w_mlp_out, g_ple,
           w_ple_gate, w_ple_up, g_final):
    B, S, D = x.shape
    depth = w_in.shape[0]
    W = w0.shape[1]
    Wm = w_branch_moba.shape[1]
    dl, al, gl = w_decay_up.shape[1], w_aaa_up.shape[1], w_gate_up.shape[1]
    assert S % MOBA_BLOCK == 0 and S % RWKV_STEP == 0
    assert W % LANES == 0 and dl <= LORA_PAD and al <= LORA_PAD and gl % LANES == 0
    M = B * S
    tiles = _tile_plan(M)
    x2 = x.reshape(M, D)
    moba_heads = Wm // MOBA_HEAD_DIM
    slopes = 2.0 ** (-8.0 * (jnp.arange(moba_heads, dtype=F32) + 1.0) / moba_heads)
    slopes = jnp.broadcast_to(slopes[:, None, None], (moba_heads, 1, MOBA_BLOCK))

    for i in range(depth):
        wi = w_in[i]
        c0 = 3 * W
        c1, c2, c3 = c0 + dl, c0 + dl + al, c0 + dl + al + gl
        tn = tiles["tn"]
        w_r = _col_tiles(jnp.concatenate([wi[:, :c0], _pad_to(wi[:, c0:c1], LORA_PAD, 1),
                                          _pad_to(wi[:, c1:c2], LORA_PAD, 1), wi[:, c2:c3]],
                                         axis=1), tn)
        w_m = _col_tiles(wi[:, c3:c3 + 3 * Wm], tn)
        w_g = _col_tiles(wi[:, c3 + 3 * Wm:], tn)
        mu = mu_shift[i]
        mu_lora = jnp.concatenate([_pad_to(mu[c0:c1], LORA_PAD, 0), _pad_to(mu[c1:c2], LORA_PAD, 0),
                                   mu[c2:c3]])[None, :]
        pvec = jnp.stack([w0[i], a0[i], k_k[i], k_a[i], r_k[i].reshape(-1), gn_w[i], gn_b[i],
                          mu[:W], mu[W:2 * W], mu[2 * W:3 * W]])
        pvec = _pad_to(pvec, 16, 0)
        wdu = _pad_to(w_decay_up[i], LORA_PAD, 0)
        wau = _pad_to(w_aaa_up[i], LORA_PAD, 0)
        gmix = g_mix[i][None, :]

        h = _norm(x2, gmix, tm=tiles["tm_norm"])
        z_r = _inproj(h, w_r, act="none", out_dtype=F32, tm=tiles["tm_in"])
        z_m = _inproj(h, w_m, act="none", out_dtype=F32, tm=tiles["tm_in"])
        gates = _inproj(h, w_g, act="sigmoid", out_dtype=BF16, tm=tiles["tm_in"])

        y_r = _rwkv(z_r.reshape(B, S, -1), pvec, mu_lora, wdu, wau, w_gate_up[i]).reshape(M, W)
        y_m = _moba(z_m, slopes, batch=B, seq=S)

        x2 = _merge_out(x2, y_r, y_m, gates, w_branch_rwkv[i].astype(BF16),
                        w_branch_moba[i].astype(BF16), w_out[i].astype(BF16), tm=tiles["tm_out"])
        x2 = _mlp(x2, g_mlp[i][None, :], _col_tiles(w_mlp_in[i], tn), w_mlp_out[i].astype(BF16),
                  tm=tiles["tm_mlp"])
        x2 = _ple(x2, p[i].reshape(M, -1), g_ple[i][None, :], g_final[None, :],
                  w_ple_gate[i].astype(BF16), w_ple_up[i].astype(BF16), tm=tiles["tm_ple"],
                  final_norm=(i == depth - 1))
    return x2.reshape(B, S, D)
```

```python
import functools
import math

import jax
import jax.numpy as jnp
from jax import lax
from jax.experimental import pallas as pl
from jax.experimental.pallas import tpu as pltpu

F32 = jnp.float32
BF16 = jnp.bfloat16

NORM_EPS = 1e-6
GN_EPS = 64e-5
NEG_INF = -1e30
LOG2E = math.log2(math.e)

LANES = 128
SUBLANES = 8
RWKV_HEAD_DIM = 64
RWKV_CHUNK = 64
RWKV_STEP = 128
RWKV_PAIRS = 4
MOBA_HEAD_DIM = 128
MOBA_BLOCK = 256
MOBA_TOPK = 3
MOBA_GROUP = 4
MOBA_ONES = 16
MOBA_UNROLL = 8
LORA_PAD = 128
VMEM_LIMIT = 56 * 1024 * 1024


def _cparams(sem):
    return pltpu.CompilerParams(dimension_semantics=sem, vmem_limit_bytes=VMEM_LIMIT)


def _rms(xf, g):
    return xf * lax.rsqrt(jnp.mean(xf * xf, axis=-1, keepdims=True) + NORM_EPS) * g


def _dot(a, b):
    return jnp.dot(a.astype(BF16), b.astype(BF16), preferred_element_type=F32)


def _split3(c):
    hi = c.astype(BF16).astype(F32)
    mid = (c - hi).astype(BF16).astype(F32)
    lo = (c - hi - mid).astype(BF16).astype(F32)
    return hi, mid, lo


def _dot_nt(a, b):
    return lax.dot_general(a.astype(BF16), b.astype(BF16), (((1,), (1,)), ((), ())),
                           preferred_element_type=F32)


def _dot_tn(a, b):
    return lax.dot_general(a.astype(BF16), b.astype(BF16), (((0,), (0,)), ((), ())),
                           preferred_element_type=F32)


def _norm_kernel(x_ref, g_ref, o_ref):
    o_ref[...] = _rms(x_ref[...], g_ref[...]).astype(o_ref.dtype)


def _norm(x2, g, *, tm):
    m, d = x2.shape
    return pl.pallas_call(
        _norm_kernel,
        grid=(m // tm,),
        in_specs=[pl.BlockSpec((tm, d), lambda i: (i, 0)),
                  pl.BlockSpec((1, d), lambda i: (0, 0))],
        out_specs=pl.BlockSpec((tm, d), lambda i: (i, 0)),
        out_shape=jax.ShapeDtypeStruct((m, d), BF16),
        compiler_params=_cparams(("arbitrary",)),
        name="mix_norm",
    )(x2, g)


def _inproj_kernel(h_ref, w_ref, o_ref, *, act):
    z = _dot(h_ref[...], w_ref[...])
    if act == "sigmoid":
        z = jax.nn.sigmoid(z)
    o_ref[...] = z.astype(o_ref.dtype)


def _col_tiles(w, tn):
    k, n = w.shape
    return w.astype(BF16).reshape(k, n // tn, tn).transpose(1, 0, 2)


def _inproj(h, w, *, act, out_dtype, tm):
    m, d = h.shape
    nt, _, tn = w.shape
    n = nt * tn
    return pl.pallas_call(
        functools.partial(_inproj_kernel, act=act),
        grid=(m // tm, nt),
        in_specs=[pl.BlockSpec((tm, d), lambda i, j: (i, 0)),
                  pl.BlockSpec((None, d, tn), lambda i, j: (j, 0, 0))],
        out_specs=pl.BlockSpec((tm, tn), lambda i, j: (i, j)),
        out_shape=jax.ShapeDtypeStruct((m, n), out_dtype),
        compiler_params=_cparams(("arbitrary", "arbitrary")),
        name="inproj_" + act,
    )(h, w)


def _head_sum(x, lane_lo):
    s0 = jnp.sum(jnp.where(lane_lo, x, 0.0), axis=-1, keepdims=True)
    s1 = jnp.sum(jnp.where(lane_lo, 0.0, x), axis=-1, keepdims=True)
    return jnp.where(lane_lo, s0, s1)


def _rwkv_kernel(r_ref, k_ref, v_ref, wd_ref, ad_ref, gd_ref, pv_ref, mul_ref,
                 wdu_ref, wau_ref, wgu_ref, o_ref, s_ref, prev_ref, prevl_ref):
    C, T, N = RWKV_CHUNK, RWKV_STEP, RWKV_HEAD_DIM
    nbatch = r_ref.shape[0]
    width = r_ref.shape[2]
    npair = width // LANES
    nchunk = T // C

    @pl.when(pl.program_id(1) == 0)
    def _():
        s_ref[...] = jnp.zeros_like(s_ref)
        prev_ref[...] = jnp.zeros_like(prev_ref)
        prevl_ref[...] = jnp.zeros_like(prevl_ref)

    row_t = lax.broadcasted_iota(jnp.int32, (T, 1), 0)

    def shift(z, prev_row, mu):
        zp = jnp.where(row_t == 0, prev_row, pltpu.roll(z, 1, 0))
        return z + mu * (zp - z)

    pv = pv_ref[...]
    w0, a0, k_k, k_a, r_k, gn_w, gn_b = (pv[i:i + 1] for i in range(7))
    mu_r, mu_k, mu_v = pv[7:8], pv[8:9], pv[9:10]
    mul = mul_ref[...]
    lane_lo = lax.broadcasted_iota(jnp.int32, (1, LANES), 1) < N

    def head_sum(x):
        return jnp.concatenate([_head_sum(x[:, q * LANES:(q + 1) * LANES], lane_lo)
                                for q in range(npair)], axis=1)

    row = lax.broadcasted_iota(jnp.int32, (2 * C, 2 * C), 0)
    col = lax.broadcasted_iota(jnp.int32, (2 * C, 2 * C), 1)
    same_head = (row // C) == (col // C)
    strict = same_head & (row > col)
    incl = same_head & (row >= col)
    eye = row == col
    pair = (row // 2) == (col // 2)
    levels = []
    b = 2
    while b < C:
        levels.append(((row // (2 * b)) == (col // (2 * b))) & ((row // b) != (col // b)))
        b *= 2
    tri = (lax.broadcasted_iota(jnp.int32, (C, C), 0)
           >= lax.broadcasted_iota(jnp.int32, (C, C), 1)).astype(F32)

    def stack(x):
        return jnp.concatenate([jnp.where(lane_lo, x, 0.0), jnp.where(lane_lo, 0.0, x)], axis=0)

    units = []
    for bi in range(nbatch):
        zr, zk, zv = r_ref[bi], k_ref[bi], v_ref[bi]
        zwd, zad, zgd = wd_ref[bi], ad_ref[bi], gd_ref[bi]
        pr = prev_ref[bi]
        pl_ = prevl_ref[bi]
        r = shift(zr, pr[0:1], mu_r)
        k = shift(zk, pr[1:2], mu_k)
        v = shift(zv, pr[2:3], mu_v)
        wd = shift(zwd, pl_[0:1, 0:LORA_PAD], mul[:, 0:LORA_PAD])
        ad = shift(zad, pl_[0:1, LORA_PAD:2 * LORA_PAD], mul[:, LORA_PAD:2 * LORA_PAD])
        gd = shift(zgd, pl_[0:1, 2 * LORA_PAD:], mul[:, 2 * LORA_PAD:])
        prev_ref[bi, 0:1, :] = zr[T - 1:T]
        prev_ref[bi, 1:2, :] = zk[T - 1:T]
        prev_ref[bi, 2:3, :] = zv[T - 1:T]
        prevl_ref[bi, 0:1, 0:LORA_PAD] = zwd[T - 1:T]
        prevl_ref[bi, 0:1, LORA_PAD:2 * LORA_PAD] = zad[T - 1:T]
        prevl_ref[bi, 0:1, 2 * LORA_PAD:] = zgd[T - 1:T]

        w_log = -jax.nn.softplus(-(w0 + _dot(jnp.tanh(wd), wdu_ref[...]))) - 0.5
        lw = -jnp.exp(w_log)
        a = jax.nn.sigmoid(a0 + _dot(ad, wau_ref[...]))
        g = _dot(jax.nn.sigmoid(gd), wgu_ref[...])
        kk = k * k_k
        kk = kk / jnp.maximum(jnp.sqrt(head_sum(kk * kk)), 1e-12)
        kmod = k * (1.0 + (a - 1.0) * k_a)
        beta = kk * a
        bonus = head_sum(r * kmod * r_k) * v

        cums = []
        for c in range(nchunk):
            parts = _dot(tri, jnp.concatenate(_split3(lw[c * C:(c + 1) * C]), axis=1))
            cums.append(parts[:, :width] + parts[:, width:2 * width] + parts[:, 2 * width:])

        for q in range(npair):
            ln = slice(q * LANES, (q + 1) * LANES)
            for c in range(nchunk):
                sl = slice(c * C, (c + 1) * C)
                units.append(dict(bi=bi, q=q, c=c, sl=sl, ln=ln, lw=lw[sl, ln], cum=cums[c][:, ln],
                                  kk=kk[sl, ln], r=r[sl, ln], beta=beta[sl, ln],
                                  kmod=kmod[sl, ln], v=v[sl, ln], bonus=bonus[sl, ln],
                                  g=g[sl, ln], gn_w=gn_w[:, ln], gn_b=gn_b[:, ln]))

    for un in units:
        cum, lw_c = un["cum"], un["lw"]
        cum_end = cum[C - 1:C]
        e_inv = jnp.exp(-cum)
        e_rem = jnp.exp(cum_end - cum)
        un["a_t"] = stack(-un["kk"] * jnp.exp(cum - lw_c))
        un["r_t"] = stack(un["r"] * jnp.exp(cum))
        un["b_h"] = un["beta"] * e_inv
        un["k_h"] = un["kmod"] * e_inv
        un["b_t"] = stack(un["beta"] * e_rem)
        un["k_t"] = stack(un["kmod"] * e_rem)
        un["v_s"] = stack(un["v"])
        un["g_col"] = jnp.sum(jnp.where(eye, jnp.exp(cum_end), 0.0), axis=1, keepdims=True)
    for un in units:
        un["scores"] = _dot_nt(jnp.concatenate([un["a_t"], un["r_t"]], axis=0),
                               jnp.concatenate([un["b_h"], un["b_h"], un["k_h"], un["k_h"]],
                                               axis=0))
    for un in units:
        sc = un.pop("scores")
        un["a_ab"] = jnp.where(strict, sc[:2 * C, :2 * C], 0.0)
        un["a_ak"] = jnp.where(strict, sc[:2 * C, 2 * C:], 0.0)
        un["a_rb"] = jnp.where(incl, sc[2 * C:, :2 * C], 0.0)
        un["a_rk"] = jnp.where(incl, sc[2 * C:, 2 * C:], 0.0)
        un["inv"] = jnp.where(eye, 1.0, jnp.where(pair, un["a_ab"], 0.0))
    for un in units:
        un["av"] = _dot(jnp.concatenate([un["a_ak"], un["a_rk"]], axis=0), un["v_s"])
    for un in units:
        un["sv"] = _dot_tn(un["k_t"], un["v_s"])
    for lvl in levels:
        for un in units:
            un["tmp"] = _dot(jnp.where(lvl, un["a_ab"], 0.0), un["inv"])
        for un in units:
            un["inv"] = un["inv"] + _dot(un["inv"], un.pop("tmp"))
    for un in units:
        wu = _dot(un["inv"], jnp.concatenate([un["a_t"], un["av"][:2 * C]], axis=1))
        un["w"], un["u0"], un["y0"] = wu[:, :LANES], wu[:, LANES:], un["av"][2 * C:]

    chains = [(bi, q) for bi in range(nbatch) for q in range(npair)]
    states = [s_ref[bi * npair + q] for bi, q in chains]
    for c in range(nchunk):
        cur = [un for un in units if un["c"] == c]
        us = [un["u0"] + _dot(un["w"], st) for un, st in zip(cur, states)]
        ys = [un["y0"] + _dot(jnp.concatenate([un["r_t"], un["a_rb"]], axis=1),
                              jnp.concatenate([st, u], axis=0))
              for un, st, u in zip(cur, states, us)]
        states = [st * un["g_col"] + un["sv"] + _dot_tn(un["b_t"], u)
                  for un, st, u in zip(cur, states, us)]
        for un, y_s in zip(cur, ys):
            y = y_s[:C] + y_s[C:]
            mean = _head_sum(y, lane_lo) * (1.0 / N)
            yc = y - mean
            var = _head_sum(yc * yc, lane_lo) * (1.0 / N)
            yn = yc * lax.rsqrt(var + GN_EPS) * un["gn_w"] + un["gn_b"]
            o_ref[un["bi"], un["sl"], un["ln"]] = ((yn + un["bonus"]) * un["g"]).astype(o_ref.dtype)
    for idx, (bi, q) in enumerate(chains):
        s_ref[bi * npair + q] = states[idx]


def _rwkv(z_r, pvec, mu_lora, wdu, wau, wgu):
    batch, seq, _ = z_r.shape
    width = wdu.shape[1]
    T = RWKV_STEP
    npair = next(n for n in (RWKV_PAIRS, 2, 1) if (width // LANES) % n == 0)
    lw = npair * LANES
    ng = width // lw
    gate_w = wgu.shape[0]
    lora_w = 2 * LORA_PAD + gate_w
    lora0 = 3 * width // LORA_PAD
    return pl.pallas_call(
        _rwkv_kernel,
        grid=(ng, seq // T),
        in_specs=[
            pl.BlockSpec((batch, T, lw), lambda h, t: (0, t, h)),
            pl.BlockSpec((batch, T, lw), lambda h, t: (0, t, ng + h)),
            pl.BlockSpec((batch, T, lw), lambda h, t: (0, t, 2 * ng + h)),
            pl.BlockSpec((batch, T, LORA_PAD), lambda h, t: (0, t, lora0)),
            pl.BlockSpec((batch, T, LORA_PAD), lambda h, t: (0, t, lora0 + 1)),
            pl.BlockSpec((batch, T, gate_w), lambda h, t: (0, t, (lora0 + 2) * LORA_PAD // gate_w)),
            pl.BlockSpec((16, lw), lambda h, t: (0, h)),
            pl.BlockSpec((1, lora_w), lambda h, t: (0, 0)),
            pl.BlockSpec((LORA_PAD, lw), lambda h, t: (0, h)),
            pl.BlockSpec((LORA_PAD, lw), lambda h, t: (0, h)),
            pl.BlockSpec((gate_w, lw), lambda h, t: (0, h)),
        ],
        out_specs=pl.BlockSpec((batch, T, lw), lambda h, t: (0, t, h)),
        out_shape=jax.ShapeDtypeStruct((batch, seq, width), BF16),
        scratch_shapes=[pltpu.VMEM((batch * npair, LANES, LANES), F32),
                        pltpu.VMEM((batch, SUBLANES, lw), F32),
                        pltpu.VMEM((batch, SUBLANES, lora_w), F32)],
        compiler_params=_cparams(("arbitrary", "arbitrary")),
        name="rwkv7_mix",
    )(z_r, z_r, z_r, z_r, z_r, z_r, pvec, mu_lora, wdu, wau, wgu)


def _moba_kernel(jj_ref, tt_ref, q_ref, k_ref, v_ref, slope_ref, o_ref, ka_ref, vt_ref, qa_ref,
                 kmean_ref, ma_ref, mb_ref, acca_ref, accb_ref, s_ref, *, nb, tile, ntrips):
    BLK, DH = MOBA_BLOCK, MOBA_HEAD_DIM
    nbp = kmean_ref.shape[0]
    neg = NEG_INF * LOG2E

    def rows(n):
        return pl.ds(pl.multiple_of(n * BLK, BLK), BLK)

    kmean_ref[...] = jnp.zeros_like(kmean_ref)
    lane = lax.broadcasted_iota(jnp.int32, (BLK, DH), 1)
    key_off = lax.broadcasted_iota(jnp.int32, (BLK, DH), 0).astype(F32)

    def prep(it, carry):
        blocks = [it * tile + r for r in range(tile)]
        kfs = [k_ref[rows(n), :] for n in blocks]
        vts = [v_ref[rows(n), :].T for n in blocks]
        for n, kf, vt in zip(blocks, kfs, vts):
            ext = jnp.where(lane < nbp, jnp.where(lane == n, 1.0, 0.0),
                            jnp.where(lane < nbp + 3, n.astype(F32),
                                      jnp.where(lane < nbp + 6, key_off, 0.0)))
            ka_ref[n] = jnp.concatenate([kf, ext], axis=1).astype(BF16)
            vt_ref[n] = jnp.concatenate([vt, jnp.ones((MOBA_ONES, BLK), F32)],
                                        axis=0).astype(BF16)
            kmean_ref[pl.ds(n, 1), :] = jnp.mean(kf, axis=0, keepdims=True)
        return carry

    lax.fori_loop(0, nb // tile, prep, 0)

    slope2 = slope_ref[...] * LOG2E
    coef = _split3(slope2 * float(BLK)) + _split3(slope2)
    erow = lax.broadcasted_iota(jnp.int32, (DH - nbp, BLK), 0)
    ext_q = jnp.zeros((DH - nbp, BLK), F32)
    for idx, cf in enumerate(coef):
        ext_q = jnp.where(erow == idx, cf, ext_q)
    blk_id = lax.broadcasted_iota(jnp.int32, (nbp, BLK), 0)
    k1, k2, k3 = _split3(kmean_ref[...])
    kmean6 = jnp.concatenate([k1, k1, k2, k1, k2, k3], axis=1)
    causal = (lax.broadcasted_iota(jnp.int32, (BLK, BLK), 1)
              >= lax.broadcasted_iota(jnp.int32, (BLK, BLK), 0))
    lane2 = lax.broadcasted_iota(jnp.int32, (BLK, 2 * DH), 1)
    not_onehot = (lane2 < DH) | (lane2 >= DH + nbp)

    def select(gate, i):
        past = blk_id < i
        gm = jnp.where(past, gate, NEG_INF)
        chosen = jnp.zeros((nbp, BLK), dtype=jnp.bool_)
        for _ in range(MOBA_TOPK):
            mx = jnp.max(gm, axis=0, keepdims=True)
            first = jnp.min(jnp.where(gm == mx, blk_id, nbp), axis=0, keepdims=True)
            pick = blk_id == first
            chosen = chosen | pick
            gm = jnp.where(pick, -jnp.inf, gm)
        return jnp.where(chosen & past, 0.0, neg)

    qtile = 2 * tile if nb % (2 * tile) == 0 else tile

    def qprep(it, carry):
        blocks = [it * qtile + r for r in range(qtile)]
        q_ts = [q_ref[rows(i), :].T for i in blocks]
        gates = []
        for q_t in q_ts:
            q1, q2, q3 = _split3(q_t)
            gates.append(_dot(kmean6, jnp.concatenate([q1, q2, q1, q3, q2, q1], axis=0)))
        selbs = [select(g, i) for g, i in zip(gates, blocks)]
        qas = [jnp.concatenate([q_t * (DH ** -0.5 * LOG2E), sb, ext_q], axis=0).astype(BF16)
               for q_t, sb in zip(q_ts, selbs)]
        for i, qa in zip(blocks, qas):
            qa_ref[i] = qa
        ss = [jnp.where(causal, _dot(jnp.where(not_onehot, ka_ref[i], 0), qa), neg)
              for i, qa in zip(blocks, qas)]
        ms = [jnp.max(s, axis=0, keepdims=True) for s in ss]
        ps = [jnp.exp2(s - m) for s, m in zip(ss, ms)]
        for i, m, p in zip(blocks, ms, ps):
            ma_ref[i] = m
            acca_ref[i] = _dot(vt_ref[i], p.astype(BF16))
            mb_ref[i] = jnp.full((1, BLK), -jnp.inf, F32)
            accb_ref[i] = jnp.zeros(accb_ref.shape[1:], F32)
        return carry

    lax.fori_loop(0, nb // qtile, qprep, 0)

    def logits_to(slot, tr):
        kaj = ka_ref[jj_ref[tr]]
        t0 = tt_ref[tr] * tile
        mxs = []
        for r in range(tile):
            s = _dot(kaj, qa_ref[t0 + r])
            s_ref[slot, r] = s
            mxs.append(jnp.max(s, axis=0, keepdims=True))
        return tuple(mxs)

    def half_trip(slot, tr, mxs, tr_next):
        m_ref, acc_ref = (ma_ref, acca_ref) if slot == 0 else (mb_ref, accb_ref)
        mxs_next = logits_to(1 - slot, tr_next)
        vtj = vt_ref[jj_ref[tr]]
        t0 = tt_ref[tr] * tile
        for r in range(tile):
            i = t0 + r
            m = m_ref[i]
            m_new = jnp.maximum(m, mxs[r])
            p = jnp.exp2(s_ref[slot, r] - m_new)
            m_ref[i] = m_new
            acc_ref[i] = acc_ref[i] * jnp.exp2(m - m_new) + _dot(vtj, p.astype(BF16))
        return mxs_next

    def body(pi, mxs):
        tr = MOBA_UNROLL * pi
        for u in range(MOBA_UNROLL - 1):
            mxs = half_trip(u % 2, tr + u, mxs, tr + u + 1)
        last = tr + MOBA_UNROLL - 1
        return half_trip((MOBA_UNROLL - 1) % 2, last, mxs, jnp.minimum(last + 1, ntrips - 1))

    if ntrips:
        lax.fori_loop(0, ntrips // MOBA_UNROLL, body, logits_to(0, 0))

    def finish(it, carry):
        blocks = [it * tile + r for r in range(tile)]
        outs = []
        for i in blocks:
            ma, mb = ma_ref[i], mb_ref[i]
            m = jnp.maximum(ma, mb)
            acc = acca_ref[i] * jnp.exp2(ma - m) + accb_ref[i] * jnp.exp2(mb - m)
            outs.append((acc[:DH] / acc[DH:DH + 1]).T)
        for i, o in zip(blocks, outs):
            o_ref[rows(i), :] = o.astype(o_ref.dtype)
        return carry

    lax.fori_loop(0, nb // tile, finish, 0)


def _moba_schedule(nb, tile):
    trips = [(j, t) for j in range(nb - 1) for t in range((j + 1) // tile, nb // tile)]
    while len(trips) % MOBA_UNROLL:
        trips.append((nb - 1, nb // tile - 1))
    return trips


def _moba(z_m, slopes, *, batch, seq):
    m, cols = z_m.shape
    heads = cols // (3 * MOBA_HEAD_DIM)
    BLK, DH = MOBA_BLOCK, MOBA_HEAD_DIM
    nb = seq // BLK
    nbp = -(-nb // SUBLANES) * SUBLANES
    assert nbp + 6 <= DH
    tile = next(g for g in (MOBA_GROUP, 2, 1) if nb % g == 0)
    trips = _moba_schedule(nb, tile)
    sched = jnp.asarray(trips or [(0, 0)], jnp.int32)
    grid_spec = pltpu.PrefetchScalarGridSpec(
        num_scalar_prefetch=2,
        grid=(batch, heads),
        in_specs=[
            pl.BlockSpec((seq, DH), lambda b, h, jj, tt: (b, h)),
            pl.BlockSpec((seq, DH), lambda b, h, jj, tt: (b, heads + h)),
            pl.BlockSpec((seq, DH), lambda b, h, jj, tt: (b, 2 * heads + h)),
            pl.BlockSpec((None, 1, BLK), lambda b, h, jj, tt: (h, 0, 0)),
        ],
        out_specs=pl.BlockSpec((seq, DH), lambda b, h, jj, tt: (b, h)),
        scratch_shapes=[pltpu.VMEM((nb, BLK, 2 * DH), BF16),
                        pltpu.VMEM((nb, DH + MOBA_ONES, BLK), BF16),
                        pltpu.VMEM((nb, 2 * DH, BLK), BF16),
                        pltpu.VMEM((nbp, DH), F32),
                        pltpu.VMEM((nb, 1, BLK), F32),
                        pltpu.VMEM((nb, 1, BLK), F32),
                        pltpu.VMEM((nb, DH + MOBA_ONES, BLK), F32),
                        pltpu.VMEM((nb, DH + MOBA_ONES, BLK), F32),
                        pltpu.VMEM((2, tile, BLK, BLK), F32)])
    return pl.pallas_call(
        functools.partial(_moba_kernel, nb=nb, tile=tile, ntrips=len(trips)),
        grid_spec=grid_spec,
        out_shape=jax.ShapeDtypeStruct((m, heads * DH), BF16),
        compiler_params=_cparams(("arbitrary", "arbitrary")),
        name="moba_attn",
    )(sched[:, 0], sched[:, 1], z_m, z_m, z_m, slopes)


def _resident(shape):
    return pl.BlockSpec(shape, lambda i: (0,) * len(shape), pipeline_mode=pl.Buffered(1))


def _merge_out_kernel(x_ref, yr_ref, ym_ref, g_ref, wr_ref, wm_ref, wo_ref, o_ref):
    d = o_ref.shape[1]
    br = _dot(yr_ref[...], wr_ref[...])
    bm = _dot(ym_ref[...], wm_ref[...])
    merged = g_ref[:, :d].astype(F32) * br + g_ref[:, d:].astype(F32) * bm
    o_ref[...] = x_ref[...] + _dot(merged, wo_ref[...])


def _merge_out(x2, y_r, y_m, gates, w_r, w_m, w_o, *, tm):
    m, d = x2.shape
    kr, km = y_r.shape[1], y_m.shape[1]
    return pl.pallas_call(
        _merge_out_kernel,
        grid=(m // tm,),
        in_specs=[pl.BlockSpec((tm, d), lambda i: (i, 0)),
                  pl.BlockSpec((tm, kr), lambda i: (i, 0)),
                  pl.BlockSpec((tm, km), lambda i: (i, 0)),
                  pl.BlockSpec((tm, 2 * d), lambda i: (i, 0)),
                  _resident((kr, d)), _resident((km, d)), _resident((d, d))],
        out_specs=pl.BlockSpec((tm, d), lambda i: (i, 0)),
        out_shape=jax.ShapeDtypeStruct((m, d), F32),
        compiler_params=_cparams(("arbitrary",)),
        name="merge_out_proj",
    )(x2, y_r, y_m, gates, w_r, w_m, w_o)


def _mlp_kernel(x_ref, g_ref, w1_ref, w2_ref, o_ref, h_ref):
    @pl.when(pl.program_id(1) == 0)
    def _():
        xf = x_ref[...]
        h_ref[...] = _rms(xf, g_ref[...]).astype(BF16)
        o_ref[...] = xf

    u = jnp.maximum(_dot(h_ref[...], w1_ref[...]), 0.0)
    o_ref[...] += _dot((u * u).astype(BF16), w2_ref[...])


def _mlp(x2, g, w1, w2, *, tm):
    m, d = x2.shape
    nf, _, tf = w1.shape
    return pl.pallas_call(
        _mlp_kernel,
        grid=(m // tm, nf),
        in_specs=[pl.BlockSpec((tm, d), lambda i, j: (i, 0)),
                  pl.BlockSpec((1, d), lambda i, j: (0, 0)),
                  pl.BlockSpec((None, d, tf), lambda i, j: (j, 0, 0)),
                  pl.BlockSpec((tf, d), lambda i, j: (j, 0))],
        out_specs=pl.BlockSpec((tm, d), lambda i, j: (i, 0)),
        out_shape=jax.ShapeDtypeStruct((m, d), F32),
        scratch_shapes=[pltpu.VMEM((tm, d), BF16)],
        compiler_params=_cparams(("arbitrary", "arbitrary")),
        name="mlp_relu2",
    )(x2, g, w1, w2)


def _ple_kernel(x_ref, p_ref, g_ref, gf_ref, wg_ref, wu_ref, o_ref, *, final_norm):
    xf = x_ref[...]
    hp = _rms(xf, g_ref[...]).astype(BF16)
    gate = jax.nn.sigmoid(_dot(hp, wg_ref[...]))
    up = _dot(p_ref[...].astype(BF16), wu_ref[...])
    y = xf + gate * up
    o_ref[...] = _rms(y, gf_ref[...]) if final_norm else y


def _ple(x2, p2, g, gf, wg, wu, *, tm, final_norm):
    m, d = x2.shape
    pd = p2.shape[1]
    return pl.pallas_call(
        functools.partial(_ple_kernel, final_norm=final_norm),
        grid=(m // tm,),
        in_specs=[pl.BlockSpec((tm, d), lambda i: (i, 0)),
                  pl.BlockSpec((tm, pd), lambda i: (i, 0)),
                  pl.BlockSpec((1, d), lambda i: (0, 0)),
                  pl.BlockSpec((1, d), lambda i: (0, 0)),
                  _resident((d, d)), _resident((pd, d))],
        out_specs=pl.BlockSpec((tm, d), lambda i: (i, 0)),
        out_shape=jax.ShapeDtypeStruct((m, d), F32),
        compiler_params=_cparams(("arbitrary",)),
        name="ple_final",
    )(x2, p2, g, gf, wg, wu)


def _tile_plan(m):
    def rows(cap):
        return next(t for t in (2048, 1024, 512, 256, 128, 64, 32, 16, 8)
                    if t <= cap and m % t == 0)

    return dict(tn=512, tm_norm=rows(512), tm_in=rows(2048), tm_out=rows(512), tm_mlp=rows(1024),
                tm_ple=rows(512))


def _pad_to(a, n, axis):
    pad = [(0, 0)] * a.ndim
    pad[axis] = (0, n - a.shape[axis])
    return jnp.pad(a, pad)


def kernel(x, p, g_mix, w_in, mu_shift, w0, w_decay_up, a0, w_aaa_up, w_gate_up, k_k, k_a, r_k,
           gn_w, gn_b, w_branch_rwkv, w_branch_moba, w_out, g_mlp, w_mlp_in, w_mlp_out, g_ple,
           w_ple_gate, w_ple_up, g_final):
    B, S, D = x.shape
    depth = w_in.shape[0]
    W = w0.shape[1]
    Wm = w_branch_moba.shape[1]
    dl, al, gl = w_decay_up.shape[1], w_aaa_up.shape[1], w_gate_up.shape[1]
    assert S % MOBA_BLOCK == 0 and S % RWKV_STEP == 0
    assert W % LANES == 0 and dl <= LORA_PAD and al <= LORA_PAD and gl % LANES == 0
    M = B * S
    tiles = _tile_plan(M)
    x2 = x.reshape(M, D)
    moba_heads = Wm // MOBA_HEAD_DIM
    slopes = 2.0 ** (-8.0 * (jnp.arange(moba_heads, dtype=F32) + 1.0) / moba_heads)
    slopes = jnp.broadcast_to(slopes[:, None, None], (moba_heads, 1, MOBA_BLOCK))

    for i in range(depth):
        wi = w_in[i]
        c0 = 3 * W
        c1, c2, c3 = c0 + dl, c0 + dl + al, c0 + dl + al + gl
        tn = tiles["tn"]
        w_r = _col_tiles(jnp.concatenate([wi[:, :c0], _pad_to(wi[:, c0:c1], LORA_PAD, 1),
                                          _pad_to(wi[:, c1:c2], LORA_PAD, 1), wi[:, c2:c3]],
                                         axis=1), tn)
        w_m = _col_tiles(wi[:, c3:c3 + 3 * Wm], tn)
        w_g = _col_tiles(wi[:, c3 + 3 * Wm:], tn)
        mu = mu_shift[i]
        mu_lora = jnp.concatenate([_pad_to(mu[c0:c1], LORA_PAD, 0), _pad_to(mu[c1:c2], LORA_PAD, 0),
                                   mu[c2:c3]])[None, :]
        pvec = jnp.stack([w0[i], a0[i], k_k[i], k_a[i], r_k[i].reshape(-1), gn_w[i], gn_b[i],
                          mu[:W], mu[W:2 * W], mu[2 * W:3 * W]])
        pvec = _pad_to(pvec, 16, 0)
        wdu = _pad_to(w_decay_up[i], LORA_PAD, 0)
        wau = _pad_to(w_aaa_up[i], LORA_PAD, 0)
        gmix = g_mix[i][None, :]

        h = _norm(x2, gmix, tm=tiles["tm_norm"])
        z_r = _inproj(h, w_r, act="none", out_dtype=F32, tm=tiles["tm_in"])
        z_m = _inproj(h, w_m, act="none", out_dtype=F32, tm=tiles["tm_in"])
        gates = _inproj(h, w_g, act="sigmoid", out_dtype=BF16, tm=tiles["tm_in"])

        y_r = _rwkv(z_r.reshape(B, S, -1), pvec, mu_lora, wdu, wau, w_gate_up[i]).reshape(M, W)
        y_m = _moba(z_m, slopes, batch=B, seq=S)

        x2 = _merge_out(x2, y_r, y_m, gates, w_branch_rwkv[i].astype(BF16),
                        w_branch_moba[i].astype(BF16), w_out[i].astype(BF16), tm=tiles["tm_out"])
        x2 = _mlp(x2, g_mlp[i][None, :], _col_tiles(w_mlp_in[i], tn), w_mlp_out[i].astype(BF16),
                  tm=tiles["tm_mlp"])
        x2 = _ple(x2, p[i].reshape(M, -1), g_ple[i][None, :], g_final[None, :],
                  w_ple_gate[i].astype(BF16), w_ple_up[i].astype(BF16), tm=tiles["tm_ple"],
                  final_norm=(i == depth - 1))
    return x2.reshape(B, S, D)
```

```python
import functools
import math

import jax
import jax.numpy as jnp
from jax import lax
from jax.experimental import pallas as pl
from jax.experimental.pallas import tpu as pltpu

F32 = jnp.float32
BF16 = jnp.bfloat16

NORM_EPS = 1e-6
GN_EPS = 64e-5
NEG_INF = -1e30
LOG2E = math.log2(math.e)

LANES = 128
SUBLANES = 8
RWKV_HEAD_DIM = 64
RWKV_CHUNK = 64
RWKV_STEP = 128
RWKV_PAIRS = 4
MOBA_HEAD_DIM = 128
MOBA_BLOCK = 256
MOBA_TOPK = 3
MOBA_GROUP = 4
MOBA_PREP = 4
MOBA_ONES = 16
MOBA_UNROLL = 8
LORA_PAD = 128
VMEM_LIMIT = 56 * 1024 * 1024


def _cparams(sem):
    return pltpu.CompilerParams(dimension_semantics=sem, vmem_limit_bytes=VMEM_LIMIT)


def _rms(xf, g):
    return xf * lax.rsqrt(jnp.mean(xf * xf, axis=-1, keepdims=True) + NORM_EPS) * g


def _sigmoid(x):
    return 0.5 * jnp.tanh(0.5 * x) + 0.5


def _dot(a, b):
    return jnp.dot(a.astype(BF16), b.astype(BF16), preferred_element_type=F32)


def _split3(c):
    hi = c.astype(BF16).astype(F32)
    mid = (c - hi).astype(BF16).astype(F32)
    lo = (c - hi - mid).astype(BF16).astype(F32)
    return hi, mid, lo


def _dot_nt(a, b):
    return lax.dot_general(a.astype(BF16), b.astype(BF16), (((1,), (1,)), ((), ())),
                           preferred_element_type=F32)


def _dot_tn(a, b):
    return lax.dot_general(a.astype(BF16), b.astype(BF16), (((0,), (0,)), ((), ())),
                           preferred_element_type=F32)


def _norm_kernel(x_ref, g_ref, o_ref):
    o_ref[...] = _rms(x_ref[...], g_ref[...]).astype(o_ref.dtype)


def _norm(x2, g, *, tm):
    m, d = x2.shape
    return pl.pallas_call(
        _norm_kernel,
        grid=(m // tm,),
        in_specs=[pl.BlockSpec((tm, d), lambda i: (i, 0)),
                  pl.BlockSpec((1, d), lambda i: (0, 0))],
        out_specs=pl.BlockSpec((tm, d), lambda i: (i, 0)),
        out_shape=jax.ShapeDtypeStruct((m, d), BF16),
        compiler_params=_cparams(("arbitrary",)),
        name="mix_norm",
    )(x2, g)


def _inproj_kernel(h_ref, w_ref, o_ref, *, act):
    z = _dot(h_ref[...], w_ref[...])
    if act == "sigmoid":
        z = _sigmoid(z)
    o_ref[...] = z.astype(o_ref.dtype)


def _col_tiles(w, tn):
    k, n = w.shape
    return w.astype(BF16).reshape(k, n // tn, tn).transpose(1, 0, 2)


def _inproj(h, w, *, act, out_dtype, tm):
    m, d = h.shape
    nt, _, tn = w.shape
    n = nt * tn
    return pl.pallas_call(
        functools.partial(_inproj_kernel, act=act),
        grid=(m // tm, nt),
        in_specs=[pl.BlockSpec((tm, d), lambda i, j: (i, 0)),
                  pl.BlockSpec((None, d, tn), lambda i, j: (j, 0, 0))],
        out_specs=pl.BlockSpec((tm, tn), lambda i, j: (i, j)),
        out_shape=jax.ShapeDtypeStruct((m, n), out_dtype),
        compiler_params=_cparams(("arbitrary", "arbitrary")),
        name="inproj_" + act,
    )(h, w)


def _head_sum(x, lane_lo):
    s0 = jnp.sum(jnp.where(lane_lo, x, 0.0), axis=-1, keepdims=True)
    s1 = jnp.sum(jnp.where(lane_lo, 0.0, x), axis=-1, keepdims=True)
    return jnp.where(lane_lo, s0, s1)


def _rwkv_kernel(r_ref, k_ref, v_ref, wd_ref, ad_ref, gd_ref, pv_ref, mul_ref,
                 wdu_ref, wau_ref, wgu_ref, o_ref, s_ref, prev_ref, prevl_ref):
    C, T, N = RWKV_CHUNK, RWKV_STEP, RWKV_HEAD_DIM
    nbatch = r_ref.shape[0]
    width = r_ref.shape[2]
    npair = width // LANES
    nchunk = T // C

    @pl.when(pl.program_id(1) == 0)
    def _():
        s_ref[...] = jnp.zeros_like(s_ref)
        prev_ref[...] = jnp.zeros_like(prev_ref)
        prevl_ref[...] = jnp.zeros_like(prevl_ref)

    row_t = lax.broadcasted_iota(jnp.int32, (T, 1), 0)

    def shift(z, prev_row, mu):
        zp = jnp.where(row_t == 0, prev_row, pltpu.roll(z, 1, 0))
        return z + mu * (zp - z)

    pv = pv_ref[...]
    w0, a0, k_k, k_a, r_k, gn_w, gn_b = (pv[i:i + 1] for i in range(7))
    mu_r, mu_k, mu_v = pv[7:8], pv[8:9], pv[9:10]
    mul = mul_ref[...]
    lane_lo = lax.broadcasted_iota(jnp.int32, (1, LANES), 1) < N

    def head_sum(x):
        return jnp.concatenate([_head_sum(x[:, q * LANES:(q + 1) * LANES], lane_lo)
                                for q in range(npair)], axis=1)

    row = lax.broadcasted_iota(jnp.int32, (2 * C, 2 * C), 0)
    col = lax.broadcasted_iota(jnp.int32, (2 * C, 2 * C), 1)
    same_head = (row // C) == (col // C)
    strict = same_head & (row > col)
    incl = same_head & (row >= col)
    eye = row == col
    pair = (row // 2) == (col // 2)
    levels = []
    b = 2
    while b < C:
        levels.append(((row // (2 * b)) == (col // (2 * b))) & ((row // b) != (col // b)))
        b *= 2
    tri = (lax.broadcasted_iota(jnp.int32, (C, C), 0)
           >= lax.broadcasted_iota(jnp.int32, (C, C), 1)).astype(F32)

    zero_b = jnp.zeros((), BF16)

    def stack(x):
        xb = x.astype(BF16)
        return jnp.concatenate([jnp.where(lane_lo, xb, zero_b), jnp.where(lane_lo, zero_b, xb)],
                               axis=0)

    def prep_rows(rows_):
        units = []
        for bi in rows_:
            zr, zk, zv = r_ref[bi], k_ref[bi], v_ref[bi]
            zwd, zad, zgd = wd_ref[bi], ad_ref[bi], gd_ref[bi]
            pr = prev_ref[bi]
            pl_ = prevl_ref[bi]
            r = shift(zr, pr[0:1], mu_r)
            k = shift(zk, pr[1:2], mu_k)
            v = shift(zv, pr[2:3], mu_v)
            wd = shift(zwd, pl_[0:1, 0:LORA_PAD], mul[:, 0:LORA_PAD])
            ad = shift(zad, pl_[0:1, LORA_PAD:2 * LORA_PAD], mul[:, LORA_PAD:2 * LORA_PAD])
            gd = shift(zgd, pl_[0:1, 2 * LORA_PAD:], mul[:, 2 * LORA_PAD:])
            prev_ref[bi, 0:1, :] = zr[T - 1:T]
            prev_ref[bi, 1:2, :] = zk[T - 1:T]
            prev_ref[bi, 2:3, :] = zv[T - 1:T]
            prevl_ref[bi, 0:1, 0:LORA_PAD] = zwd[T - 1:T]
            prevl_ref[bi, 0:1, LORA_PAD:2 * LORA_PAD] = zad[T - 1:T]
            prevl_ref[bi, 0:1, 2 * LORA_PAD:] = zgd[T - 1:T]

            w_log = -jax.nn.softplus(-(w0 + _dot(jnp.tanh(wd), wdu_ref[...]))) - 0.5
            lw = -jnp.exp(w_log)
            a = _sigmoid(a0 + _dot(ad, wau_ref[...]))
            g = _dot(_sigmoid(gd), wgu_ref[...])
            kk = k * k_k
            kk = kk / jnp.maximum(jnp.sqrt(head_sum(kk * kk)), 1e-12)
            kmod = k * (1.0 + (a - 1.0) * k_a)
            beta = kk * a
            bonus = head_sum(r * kmod * r_k) * v

            cums = []
            for c in range(nchunk):
                parts = _dot(tri, jnp.concatenate(_split3(lw[c * C:(c + 1) * C]), axis=1))
                cums.append(parts[:, :width] + parts[:, width:2 * width] + parts[:, 2 * width:])

            for q in range(npair):
                ln = slice(q * LANES, (q + 1) * LANES)
                for c in range(nchunk):
                    sl = slice(c * C, (c + 1) * C)
                    units.append(dict(bi=bi, q=q, c=c, sl=sl, ln=ln, lw=lw[sl, ln],
                                      cum=cums[c][:, ln], kk=kk[sl, ln], r=r[sl, ln],
                                      beta=beta[sl, ln], kmod=kmod[sl, ln], v=v[sl, ln],
                                      bonus=bonus[sl, ln], g=g[sl, ln], gn_w=gn_w[:, ln],
                                      gn_b=gn_b[:, ln]))
        return units

    def st_operands(units):
        for un in units:
            cum, lw_c = un["cum"], un["lw"]
            cum_end = cum[C - 1:C]
            e_inv = jnp.exp(-cum)
            e_rem = jnp.exp(cum_end - cum)
            un["a_t"] = stack(-un["kk"] * jnp.exp(cum - lw_c))
            un["r_t"] = stack(un["r"] * jnp.exp(cum))
            un["b_h"] = (un["beta"] * e_inv).astype(BF16)
            un["k_h"] = (un["kmod"] * e_inv).astype(BF16)
            un["b_t"] = stack(un["beta"] * e_rem)
            un["k_t"] = stack(un["kmod"] * e_rem)
            un["v_s"] = stack(un["v"])
            un["g_col"] = jnp.sum(jnp.where(eye, jnp.exp(cum_end), 0.0), axis=1, keepdims=True)
        for un in units:
            un["scores"] = _dot_nt(jnp.concatenate([un["a_t"], un["r_t"]], axis=0),
                                   jnp.concatenate([un["b_h"], un["b_h"], un["k_h"], un["k_h"]],
                                                   axis=0))

    def st_couplings(units):
        for un in units:
            sc = un.pop("scores")
            a_ab = jnp.where(strict, sc[:2 * C, :2 * C], 0.0)
            un["a_ab"] = a_ab.astype(BF16)
            un["a_ak"] = jnp.where(strict, sc[:2 * C, 2 * C:], 0.0).astype(BF16)
            un["a_rb"] = jnp.where(incl, sc[2 * C:, :2 * C], 0.0).astype(BF16)
            un["a_rk"] = jnp.where(incl, sc[2 * C:, 2 * C:], 0.0).astype(BF16)
            un["inv"] = jnp.where(eye, 1.0, jnp.where(pair, a_ab, 0.0))
        for un in units:
            un["av"] = _dot(jnp.concatenate([un["a_ak"], un["a_rk"]], axis=0), un["v_s"])
        for un in units:
            un["sv"] = _dot_tn(un["k_t"], un["v_s"])

    def st_level(units, lvl):
        for un in units:
            un["inv_b"] = un["inv"].astype(BF16)
            un["tmp"] = _dot(jnp.where(lvl, un["a_ab"], zero_b), un["inv_b"])
        for un in units:
            un["inv"] = un["inv"] + _dot(un.pop("inv_b"), un.pop("tmp"))

    def st_solve(units):
        for un in units:
            av = un.pop("av")
            wu = _dot(un["inv"], jnp.concatenate([un["a_t"], av[:2 * C].astype(BF16)], axis=1))
            un["w"], un["u0"], un["y0"] = wu[:, :LANES].astype(BF16), wu[:, LANES:], av[2 * C:]

    def phase_b(units):
        chains = sorted({(un["bi"], un["q"]) for un in units})
        states = [s_ref[bi * npair + q] for bi, q in chains]
        for c in range(nchunk):
            cur = [un for un in units if un["c"] == c]
            us = [un["u0"] + _dot(un["w"], st) for un, st in zip(cur, states)]
            ys = [un["y0"] + _dot(jnp.concatenate([un["r_t"], un["a_rb"]], axis=1),
                                  jnp.concatenate([st.astype(BF16), u.astype(BF16)], axis=0))
                  for un, st, u in zip(cur, states, us)]
            states = [st * un["g_col"] + un["sv"] + _dot_tn(un["b_t"], u)
                      for un, st, u in zip(cur, states, us)]
            for un, y_s in zip(cur, ys):
                y = y_s[:C] + y_s[C:]
                mean = _head_sum(y, lane_lo) * (1.0 / N)
                yc = y - mean
                var = _head_sum(yc * yc, lane_lo) * (1.0 / N)
                yn = yc * lax.rsqrt(var + GN_EPS) * un["gn_w"] + un["gn_b"]
                o_ref[un["bi"], un["sl"], un["ln"]] = (
                    (yn + un["bonus"]) * un["g"]).astype(o_ref.dtype)
        for (bi, q), st in zip(chains, states):
            s_ref[bi * npair + q] = st

    units = prep_rows(range(nbatch))
    st_operands(units)
    st_couplings(units)
    for lvl in levels:
        st_level(units, lvl)
    st_solve(units)
    phase_b(units)


def _rwkv(z_r, pvec, mu_lora, wdu, wau, wgu):
    batch, seq, _ = z_r.shape
    width = wdu.shape[1]
    T = RWKV_STEP
    npair = next(n for n in (RWKV_PAIRS, 4, 2, 1) if (width // LANES) % n == 0)
    lw = npair * LANES
    ng = width // lw
    gate_w = wgu.shape[0]
    lora_w = 2 * LORA_PAD + gate_w
    lora0 = 3 * width // LORA_PAD
    return pl.pallas_call(
        _rwkv_kernel,
        grid=(ng, seq // T),
        in_specs=[
            pl.BlockSpec((batch, T, lw), lambda h, t: (0, t, h)),
            pl.BlockSpec((batch, T, lw), lambda h, t: (0, t, ng + h)),
            pl.BlockSpec((batch, T, lw), lambda h, t: (0, t, 2 * ng + h)),
            pl.BlockSpec((batch, T, LORA_PAD), lambda h, t: (0, t, lora0)),
            pl.BlockSpec((batch, T, LORA_PAD), lambda h, t: (0, t, lora0 + 1)),
            pl.BlockSpec((batch, T, gate_w), lambda h, t: (0, t, (lora0 + 2) * LORA_PAD // gate_w)),
            pl.BlockSpec((16, lw), lambda h, t: (0, h)),
            pl.BlockSpec((1, lora_w), lambda h, t: (0, 0)),
            pl.BlockSpec((LORA_PAD, lw), lambda h, t: (0, h)),
            pl.BlockSpec((LORA_PAD, lw), lambda h, t: (0, h)),
            pl.BlockSpec((gate_w, lw), lambda h, t: (0, h)),
        ],
        out_specs=pl.BlockSpec((batch, T, lw), lambda h, t: (0, t, h)),
        out_shape=jax.ShapeDtypeStruct((batch, seq, width), BF16),
        scratch_shapes=[pltpu.VMEM((batch * npair, LANES, LANES), F32),
                        pltpu.VMEM((batch, SUBLANES, lw), F32),
                        pltpu.VMEM((batch, SUBLANES, lora_w), F32)],
        compiler_params=_cparams(("arbitrary", "arbitrary")),
        name="rwkv7_mix",
    )(z_r, z_r, z_r, z_r, z_r, z_r, pvec, mu_lora, wdu, wau, wgu)


def _moba_kernel(jj_ref, tt_ref, q_ref, k_ref, v_ref, slope_ref, o_ref, ka_ref, vt_ref, qa_ref,
                 kmean_ref, ma_ref, mb_ref, acca_ref, accb_ref, s_ref, *, nb, tile, ptile, ntrips):
    BLK, DH = MOBA_BLOCK, MOBA_HEAD_DIM
    nbp = kmean_ref.shape[0]
    neg = NEG_INF * LOG2E

    def rows(n):
        return pl.ds(pl.multiple_of(n * BLK, BLK), BLK)

    kmean_ref[...] = jnp.zeros_like(kmean_ref)
    lane = lax.broadcasted_iota(jnp.int32, (BLK, DH), 1)
    key_off = lax.broadcasted_iota(jnp.int32, (BLK, DH), 0).astype(F32)

    def prep(it, carry):
        blocks = [it * ptile + r for r in range(ptile)]
        kfs = [k_ref[rows(n), :] for n in blocks]
        vts = [v_ref[rows(n), :].T for n in blocks]
        for n, kf, vt in zip(blocks, kfs, vts):
            ext = jnp.where(lane < nbp, jnp.where(lane == n, 1.0, 0.0),
                            jnp.where(lane < nbp + 3, lax.convert_element_type(n, F32),
                                      jnp.where(lane < nbp + 6, key_off, 0.0)))
            ka_ref[n] = jnp.concatenate([kf, ext], axis=1).astype(BF16)
            vt_ref[n] = jnp.concatenate([vt, jnp.ones((MOBA_ONES, BLK), F32)],
                                        axis=0).astype(BF16)
            kmean_ref[pl.ds(n, 1), :] = jnp.mean(kf, axis=0, keepdims=True)
        return carry

    lax.fori_loop(0, nb // ptile, prep, 0)

    slope2 = slope_ref[...] * LOG2E
    coef = _split3(slope2 * float(BLK)) + _split3(slope2)
    erow = lax.broadcasted_iota(jnp.int32, (DH - nbp, BLK), 0)
    ext_q = jnp.zeros((DH - nbp, BLK), F32)
    for idx, cf in enumerate(coef):
        ext_q = jnp.where(erow == idx, cf, ext_q)
    blk_id = lax.broadcasted_iota(jnp.int32, (nbp, BLK), 0)
    k1, k2, k3 = _split3(kmean_ref[...])
    kmean6 = jnp.concatenate([k1, k1, k2, k1, k2, k3], axis=1)
    causal = (lax.broadcasted_iota(jnp.int32, (BLK, BLK), 1)
              >= lax.broadcasted_iota(jnp.int32, (BLK, BLK), 0))
    lane2 = lax.broadcasted_iota(jnp.int32, (BLK, 2 * DH), 1)
    not_onehot = (lane2 < DH) | (lane2 >= DH + nbp)

    def select(gate, i):
        past = blk_id < i
        gm = jnp.where(past, gate, NEG_INF)
        chosen = jnp.zeros((nbp, BLK), dtype=jnp.bool_)
        for _ in range(MOBA_TOPK):
            mx = jnp.max(gm, axis=0, keepdims=True)
            first = jnp.min(jnp.where(gm == mx, blk_id, nbp), axis=0, keepdims=True)
            pick = blk_id == first
            chosen = chosen | pick
            gm = jnp.where(pick, -jnp.inf, gm)
        return jnp.where(chosen & past, 0.0, neg)

    qtile = 2 * ptile if nb % (2 * ptile) == 0 else ptile

    def qprep(it, carry):
        blocks = [it * qtile + r for r in range(qtile)]
        q_ts = [q_ref[rows(i), :].T for i in blocks]
        gates = []
        for q_t in q_ts:
            q1, q2, q3 = _split3(q_t)
            gates.append(_dot(kmean6, jnp.concatenate([q1, q2, q1, q3, q2, q1], axis=0)))
        selbs = [select(g, i) for g, i in zip(gates, blocks)]
        qas = [jnp.concatenate([q_t * (DH ** -0.5 * LOG2E), sb, ext_q], axis=0).astype(BF16)
               for q_t, sb in zip(q_ts, selbs)]
        for i, qa in zip(blocks, qas):
            qa_ref[i] = qa
        ss = [jnp.where(causal, _dot(jnp.where(not_onehot, ka_ref[i], 0), qa), neg)
              for i, qa in zip(blocks, qas)]
        ms = [jnp.max(s, axis=0, keepdims=True) for s in ss]
        ps = [jnp.exp2(s - m) for s, m in zip(ss, ms)]
        for i, m, p in zip(blocks, ms, ps):
            ma_ref[i] = m
            acca_ref[i] = _dot(vt_ref[i], p.astype(BF16))
            mb_ref[i] = jnp.full((1, BLK), -jnp.inf, F32)
            accb_ref[i] = jnp.zeros(accb_ref.shape[1:], F32)
        return carry

    lax.fori_loop(0, nb // qtile, qprep, 0)

    def logits_to(slot, tr):
        kaj = ka_ref[jj_ref[tr]]
        t0 = tt_ref[tr] * tile
        mxs = []
        for r in range(tile):
            s = _dot(kaj, qa_ref[t0 + r])
            s_ref[slot, r] = s
            mxs.append(jnp.max(s, axis=0, keepdims=True))
        return tuple(mxs)

    def half_trip(slot, tr, mxs, tr_next):
        m_ref, acc_ref = (ma_ref, acca_ref) if slot == 0 else (mb_ref, accb_ref)
        mxs_next = logits_to(1 - slot, tr_next)
        vtj = vt_ref[jj_ref[tr]]
        t0 = tt_ref[tr] * tile
        for r in range(tile):
            i = t0 + r
            m = m_ref[i]
            m_new = jnp.maximum(m, mxs[r])
            p = jnp.exp2(s_ref[slot, r] - m_new)
            m_ref[i] = m_new
            acc_ref[i] = acc_ref[i] * jnp.exp2(m - m_new) + _dot(vtj, p.astype(BF16))
        return mxs_next

    def body(pi, mxs):
        tr = MOBA_UNROLL * pi
        for u in range(MOBA_UNROLL - 1):
            mxs = half_trip(u % 2, tr + u, mxs, tr + u + 1)
        last = tr + MOBA_UNROLL - 1
        return half_trip((MOBA_UNROLL - 1) % 2, last, mxs, jnp.minimum(last + 1, ntrips - 1))

    if ntrips:
        lax.fori_loop(0, ntrips // MOBA_UNROLL, body, logits_to(0, 0))

    def finish(it, carry):
        blocks = [it * ptile + r for r in range(ptile)]
        outs = []
        for i in blocks:
            ma, mb = ma_ref[i], mb_ref[i]
            m = jnp.maximum(ma, mb)
            acc = acca_ref[i] * jnp.exp2(ma - m) + accb_ref[i] * jnp.exp2(mb - m)
            outs.append((acc[:DH] / acc[DH:DH + 1]).T)
        for i, o in zip(blocks, outs):
            o_ref[rows(i), :] = o.astype(o_ref.dtype)
        return carry

    lax.fori_loop(0, nb // ptile, finish, 0)


def _moba_schedule(nb, tile):
    trips = [(j, t) for j in range(nb - 1) for t in range((j + 1) // tile, nb // tile)]
    while len(trips) % MOBA_UNROLL:
        trips.append((nb - 1, nb // tile - 1))
    return trips


def _moba(z_m, slopes, *, batch, seq):
    m, cols = z_m.shape
    heads = cols // (3 * MOBA_HEAD_DIM)
    BLK, DH = MOBA_BLOCK, MOBA_HEAD_DIM
    nb = seq // BLK
    nbp = -(-nb // SUBLANES) * SUBLANES
    assert nbp + 6 <= DH
    tile = next(g for g in (MOBA_GROUP, 2, 1) if nb % g == 0)
    ptile = next(g for g in (MOBA_PREP, 2, 1) if nb % g == 0)
    trips = _moba_schedule(nb, tile)
    sched = jnp.asarray(trips or [(0, 0)], jnp.int32)
    grid_spec = pltpu.PrefetchScalarGridSpec(
        num_scalar_prefetch=2,
        grid=(batch, heads),
        in_specs=[
            pl.BlockSpec((seq, DH), lambda b, h, jj, tt: (b, h)),
            pl.BlockSpec((seq, DH), lambda b, h, jj, tt: (b, heads + h)),
            pl.BlockSpec((seq, DH), lambda b, h, jj, tt: (b, 2 * heads + h)),
            pl.BlockSpec((None, 1, BLK), lambda b, h, jj, tt: (h, 0, 0)),
        ],
        out_specs=pl.BlockSpec((seq, DH), lambda b, h, jj, tt: (b, h)),
        scratch_shapes=[pltpu.VMEM((nb, BLK, 2 * DH), BF16),
                        pltpu.VMEM((nb, DH + MOBA_ONES, BLK), BF16),
                        pltpu.VMEM((nb, 2 * DH, BLK), BF16),
                        pltpu.VMEM((nbp, DH), F32),
                        pltpu.VMEM((nb, 1, BLK), F32),
                        pltpu.VMEM((nb, 1, BLK), F32),
                        pltpu.VMEM((nb, DH + MOBA_ONES, BLK), F32),
                        pltpu.VMEM((nb, DH + MOBA_ONES, BLK), F32),
                        pltpu.VMEM((2, tile, BLK, BLK), F32)])
    return pl.pallas_call(
        functools.partial(_moba_kernel, nb=nb, tile=tile, ptile=ptile, ntrips=len(trips)),
        grid_spec=grid_spec,
        out_shape=jax.ShapeDtypeStruct((m, heads * DH), BF16),
        compiler_params=_cparams(("arbitrary", "arbitrary")),
        name="moba_attn",
    )(sched[:, 0], sched[:, 1], z_m, z_m, z_m, slopes)


def _resident(shape):
    return pl.BlockSpec(shape, lambda i: (0,) * len(shape), pipeline_mode=pl.Buffered(1))


def _merge_out_kernel(x_ref, yr_ref, ym_ref, g_ref, wr_ref, wm_ref, wo_ref, o_ref):
    d = o_ref.shape[1]
    br = _dot(yr_ref[...], wr_ref[...])
    bm = _dot(ym_ref[...], wm_ref[...])
    merged = (_sigmoid(g_ref[:, :d].astype(F32)) * br
              + _sigmoid(g_ref[:, d:].astype(F32)) * bm)
    o_ref[...] = x_ref[...] + _dot(merged, wo_ref[...])


def _merge_out(x2, y_r, y_m, gates, w_r, w_m, w_o, *, tm):
    m, d = x2.shape
    kr, km = y_r.shape[1], y_m.shape[1]
    return pl.pallas_call(
        _merge_out_kernel,
        grid=(m // tm,),
        in_specs=[pl.BlockSpec((tm, d), lambda i: (i, 0)),
                  pl.BlockSpec((tm, kr), lambda i: (i, 0)),
                  pl.BlockSpec((tm, km), lambda i: (i, 0)),
                  pl.BlockSpec((tm, 2 * d), lambda i: (i, 0)),
                  _resident((kr, d)), _resident((km, d)), _resident((d, d))],
        out_specs=pl.BlockSpec((tm, d), lambda i: (i, 0)),
        out_shape=jax.ShapeDtypeStruct((m, d), F32),
        compiler_params=_cparams(("arbitrary",)),
        name="merge_out_proj",
    )(x2, y_r, y_m, gates, w_r, w_m, w_o)


def _mlp_kernel(x_ref, g_ref, w1_ref, w2_ref, o_ref, h_ref):
    @pl.when(pl.program_id(1) == 0)
    def _():
        xf = x_ref[...]
        h_ref[...] = _rms(xf, g_ref[...]).astype(BF16)
        o_ref[...] = xf

    u = jnp.maximum(_dot(h_ref[...], w1_ref[...]), 0.0)
    o_ref[...] += _dot((u * u).astype(BF16), w2_ref[...])


def _mlp(x2, g, w1, w2, *, tm):
    m, d = x2.shape
    nf, _, tf = w1.shape
    return pl.pallas_call(
        _mlp_kernel,
        grid=(m // tm, nf),
        in_specs=[pl.BlockSpec((tm, d), lambda i, j: (i, 0)),
                  pl.BlockSpec((1, d), lambda i, j: (0, 0)),
                  pl.BlockSpec((None, d, tf), lambda i, j: (j, 0, 0)),
                  pl.BlockSpec((tf, d), lambda i, j: (j, 0))],
        out_specs=pl.BlockSpec((tm, d), lambda i, j: (i, 0)),
        out_shape=jax.ShapeDtypeStruct((m, d), F32),
        scratch_shapes=[pltpu.VMEM((tm, d), BF16)],
        compiler_params=_cparams(("arbitrary", "arbitrary")),
        name="mlp_relu2",
    )(x2, g, w1, w2)


def _ple_kernel(x_ref, p_ref, g_ref, gf_ref, wg_ref, wu_ref, o_ref, *, final_norm):
    xf = x_ref[...]
    hp = _rms(xf, g_ref[...]).astype(BF16)
    gate = _sigmoid(_dot(hp, wg_ref[...]))
    up = _dot(p_ref[...].astype(BF16), wu_ref[...])
    y = xf + gate * up
    o_ref[...] = _rms(y, gf_ref[...]) if final_norm else y


def _ple(x2, p2, g, gf, wg, wu, *, tm, final_norm):
    m, d = x2.shape
    pd = p2.shape[1]
    return pl.pallas_call(
        functools.partial(_ple_kernel, final_norm=final_norm),
        grid=(m // tm,),
        in_specs=[pl.BlockSpec((tm, d), lambda i: (i, 0)),
                  pl.BlockSpec((tm, pd), lambda i: (i, 0)),
                  pl.BlockSpec((1, d), lambda i: (0, 0)),
                  pl.BlockSpec((1, d), lambda i: (0, 0)),
                  _resident((d, d)), _resident((pd, d))],
        out_specs=pl.BlockSpec((tm, d), lambda i: (i, 0)),
        out_shape=jax.ShapeDtypeStruct((m, d), F32),
        compiler_params=_cparams(("arbitrary",)),
        name="ple_final",
    )(x2, p2, g, gf, wg, wu)


def _tile_plan(m):
    def rows(cap):
        return next(t for t in (2048, 1024, 512, 256, 128, 64, 32, 16, 8)
                    if t <= cap and m % t == 0)

    return dict(tn=512, tm_norm=rows(512), tm_in=rows(2048), tm_out=rows(512), tm_mlp=rows(1024),
                tm_ple=rows(512))


def _pad_to(a, n, axis):
    pad = [(0, 0)] * a.ndim
    pad[axis] = (0, n - a.shape[axis])
    return jnp.pad(a, pad)


def kernel(x, p, g_mix, w_in, mu_shift, w0, w_decay_up, a0, w_aaa_up, w_gate_up, k_k, k_a, r_k,
           gn_w, gn_b, w_branch_rwkv, w_branch_moba, w_out, g_mlp, w_mlp_in, w_mlp_out, g_ple,
           w_ple_gate, w_ple_up, g_final):
    B, S, D = x.shape
    depth = w_in.shape[0]
    W = w0.shape[1]
    Wm = w_branch_moba.shape[1]
    dl, al, gl = w_decay_up.shape[1], w_aaa_up.shape[1], w_gate_up.shape[1]
    assert S % MOBA_BLOCK == 0 and S % RWKV_STEP == 0
    assert W % LANES == 0 and dl <= LORA_PAD and al <= LORA_PAD and gl % LANES == 0
    M = B * S
    tiles = _tile_plan(M)
    x2 = x.reshape(M, D)
    moba_heads = Wm // MOBA_HEAD_DIM
    slopes = 2.0 ** (-8.0 * (jnp.arange(moba_heads, dtype=F32) + 1.0) / moba_heads)
    slopes = jnp.broadcast_to(slopes[:, None, None], (moba_heads, 1, MOBA_BLOCK))

    for i in range(depth):
        wi = w_in[i]
        c0 = 3 * W
        c1, c2, c3 = c0 + dl, c0 + dl + al, c0 + dl + al + gl
        tn = tiles["tn"]
        w_r = _col_tiles(jnp.concatenate([wi[:, :c0], _pad_to(wi[:, c0:c1], LORA_PAD, 1),
                                          _pad_to(wi[:, c1:c2], LORA_PAD, 1), wi[:, c2:c3]],
                                         axis=1), tn)
        w_m = _col_tiles(wi[:, c3:c3 + 3 * Wm], tn)
        w_g = _col_tiles(wi[:, c3 + 3 * Wm:], tn)
        mu = mu_shift[i]
        mu_lora = jnp.concatenate([_pad_to(mu[c0:c1], LORA_PAD, 0), _pad_to(mu[c1:c2], LORA_PAD, 0),
                                   mu[c2:c3]])[None, :]
        pvec = jnp.stack([w0[i], a0[i], k_k[i], k_a[i], r_k[i].reshape(-1), gn_w[i], gn_b[i],
                          mu[:W], mu[W:2 * W], mu[2 * W:3 * W]])
        pvec = _pad_to(pvec, 16, 0)
        wdu = _pad_to(w_decay_up[i], LORA_PAD, 0)
        wau = _pad_to(w_aaa_up[i], LORA_PAD, 0)
        gmix = g_mix[i][None, :]

        h = _norm(x2, gmix, tm=tiles["tm_norm"])
        z_r = _inproj(h, w_r, act="none", out_dtype=F32, tm=tiles["tm_in"])
        z_m = _inproj(h, w_m, act="none", out_dtype=F32, tm=tiles["tm_in"])
        gates = _inproj(h, w_g, act="none", out_dtype=BF16, tm=tiles["tm_in"])

        y_r = _rwkv(z_r.reshape(B, S, -1), pvec, mu_lora, wdu, wau, w_gate_up[i]).reshape(M, W)
        y_m = _moba(z_m, slopes, batch=B, seq=S)

        x2 = _merge_out(x2, y_r, y_m, gates, w_branch_rwkv[i].astype(BF16),
                        w_branch_moba[i].astype(BF16), w_out[i].astype(BF16), tm=tiles["tm_out"])
        x2 = _mlp(x2, g_mlp[i][None, :], _col_tiles(w_mlp_in[i], tn), w_mlp_out[i].astype(BF16),
                  tm=tiles["tm_mlp"])
        x2 = _ple(x2, p[i].reshape(M, -1), g_ple[i][None, :], g_final[None, :],
                  w_ple_gate[i].astype(BF16), w_ple_up[i].astype(BF16), tm=tiles["tm_ple"],
                  final_norm=(i == depth - 1))
    return x2.reshape(B, S, D)
```

```python
import functools
import math

import jax
import jax.numpy as jnp
from jax import lax
from jax.experimental import pallas as pl
from jax.experimental.pallas import tpu as pltpu

F32 = jnp.float32
BF16 = jnp.bfloat16

NORM_EPS = 1e-6
GN_EPS = 64e-5
NEG_INF = -1e30
LOG2E = math.log2(math.e)

LANES = 128
SUBLANES = 8
RWKV_HEAD_DIM = 64
RWKV_CHUNK = 64
RWKV_STEP = 128
RWKV_PAIRS = 4
MOBA_HEAD_DIM = 128
MOBA_BLOCK = 256
MOBA_TOPK = 3
MOBA_GROUP = 4
MOBA_PREP = 4
MOBA_ONES = 16
MOBA_UNROLL = 8
LORA_PAD = 128
RWKV_PARAM_ROWS = 16
SPLIT = 3
VMEM_LIMIT = 56 * 1024 * 1024


def _cparams(sem):
    return pltpu.CompilerParams(dimension_semantics=sem, vmem_limit_bytes=VMEM_LIMIT)


def _rms(xf, g):
    return xf * lax.rsqrt(jnp.mean(xf * xf, axis=-1, keepdims=True) + NORM_EPS) * g


def _sigmoid(x):
    return 0.5 * jnp.tanh(0.5 * x) + 0.5


def _dot(a, b):
    return jnp.dot(a.astype(BF16), b.astype(BF16), preferred_element_type=F32)


def _split3(c):
    hi = c.astype(BF16).astype(F32)
    mid = (c - hi).astype(BF16).astype(F32)
    lo = (c - hi - mid).astype(BF16).astype(F32)
    return hi, mid, lo


def _dot_nt(a, b):
    return lax.dot_general(a.astype(BF16), b.astype(BF16), (((1,), (1,)), ((), ())),
                           preferred_element_type=F32)


def _dot_tn(a, b):
    return lax.dot_general(a.astype(BF16), b.astype(BF16), (((0,), (0,)), ((), ())),
                           preferred_element_type=F32)


def _norm_kernel(x_ref, g_ref, o_ref):
    o_ref[...] = _rms(x_ref[...], g_ref[...]).astype(o_ref.dtype)


def _norm(x2, g, *, tm):
    m, d = x2.shape
    return pl.pallas_call(
        _norm_kernel,
        grid=(m // tm,),
        in_specs=[pl.BlockSpec((tm, d), lambda i: (i, 0)),
                  pl.BlockSpec((1, d), lambda i: (0, 0))],
        out_specs=pl.BlockSpec((tm, d), lambda i: (i, 0)),
        out_shape=jax.ShapeDtypeStruct((m, d), BF16),
        compiler_params=_cparams(("arbitrary",)),
        name="mix_norm",
    )(x2, g)


def _inproj_kernel(h_ref, w_ref, o_ref, *, act):
    z = _dot(h_ref[...], w_ref[...])
    if act == "sigmoid":
        z = _sigmoid(z)
    o_ref[...] = z.astype(o_ref.dtype)


def _col_tiles(w, tn):
    k, n = w.shape
    return w.astype(BF16).reshape(k, n // tn, tn).transpose(1, 0, 2)


def _inproj(h, w, tile0, nt, *, act, out_dtype, tm):
    m, d = h.shape
    tn = w.shape[2]
    n = nt * tn
    return pl.pallas_call(
        functools.partial(_inproj_kernel, act=act),
        grid=(m // tm, nt),
        in_specs=[pl.BlockSpec((tm, d), lambda i, j: (i, 0)),
                  pl.BlockSpec((None, d, tn), lambda i, j: (tile0 + j, 0, 0))],
        out_specs=pl.BlockSpec((tm, tn), lambda i, j: (i, j)),
        out_shape=jax.ShapeDtypeStruct((m, n), out_dtype),
        compiler_params=_cparams(("arbitrary", "arbitrary")),
        name="inproj_" + act,
    )(h, w)


def _head_sum(x, lane_lo):
    s0 = jnp.sum(jnp.where(lane_lo, x, 0.0), axis=-1, keepdims=True)
    s1 = jnp.sum(jnp.where(lane_lo, 0.0, x), axis=-1, keepdims=True)
    return jnp.where(lane_lo, s0, s1)


def _rwkv_kernel(r_ref, k_ref, v_ref, wd_ref, ad_ref, gd_ref, pv_ref, mul_ref,
                 wdu_ref, wau_ref, wgu_ref, o_ref, s_ref, prev_ref, prevl_ref):
    C, T, N = RWKV_CHUNK, RWKV_STEP, RWKV_HEAD_DIM
    nbatch = r_ref.shape[0]
    width = r_ref.shape[2]
    npair = width // LANES
    nchunk = T // C

    @pl.when(pl.program_id(1) == 0)
    def _():
        s_ref[...] = jnp.zeros_like(s_ref)
        prev_ref[...] = jnp.zeros_like(prev_ref)
        prevl_ref[...] = jnp.zeros_like(prevl_ref)

    row_t = lax.broadcasted_iota(jnp.int32, (T, 1), 0)

    def shift(z, prev_row, mu):
        zp = jnp.where(row_t == 0, prev_row, pltpu.roll(z, 1, 0))
        return z + mu * (zp - z)

    pv = pv_ref[...]
    w0, a0, k_k, k_a, r_k, gn_w, gn_b = (pv[i:i + 1] for i in range(7))
    mu_r, mu_k, mu_v = pv[7:8], pv[8:9], pv[9:10]
    mul = mul_ref[...]
    lane_lo = lax.broadcasted_iota(jnp.int32, (1, LANES), 1) < N

    def head_sum(x):
        return jnp.concatenate([_head_sum(x[:, q * LANES:(q + 1) * LANES], lane_lo)
                                for q in range(npair)], axis=1)

    row = lax.broadcasted_iota(jnp.int32, (2 * C, 2 * C), 0)
    col = lax.broadcasted_iota(jnp.int32, (2 * C, 2 * C), 1)
    same_head = (row // C) == (col // C)
    strict = same_head & (row > col)
    incl = same_head & (row >= col)
    eye = row == col
    pair = (row // 2) == (col // 2)
    levels = []
    b = 2
    while b < C:
        levels.append(((row // (2 * b)) == (col // (2 * b))) & ((row // b) != (col // b)))
        b *= 2
    tri = (lax.broadcasted_iota(jnp.int32, (C, C), 0)
           >= lax.broadcasted_iota(jnp.int32, (C, C), 1)).astype(F32)

    zero_b = jnp.zeros((), BF16)

    def stack(x):
        xb = x.astype(BF16)
        return jnp.concatenate([jnp.where(lane_lo, xb, zero_b), jnp.where(lane_lo, zero_b, xb)],
                               axis=0)

    def prep_rows(rows_):
        units = []
        for bi in rows_:
            zr, zk, zv = r_ref[bi], k_ref[bi], v_ref[bi]
            zwd, zad, zgd = wd_ref[bi], ad_ref[bi], gd_ref[bi]
            pr = prev_ref[bi]
            pl_ = prevl_ref[bi]
            r = shift(zr, pr[0:1], mu_r)
            k = shift(zk, pr[1:2], mu_k)
            v = shift(zv, pr[2:3], mu_v)
            wd = shift(zwd, pl_[0:1, 0:LORA_PAD], mul[:, 0:LORA_PAD])
            ad = shift(zad, pl_[0:1, LORA_PAD:2 * LORA_PAD], mul[:, LORA_PAD:2 * LORA_PAD])
            gd = shift(zgd, pl_[0:1, 2 * LORA_PAD:], mul[:, 2 * LORA_PAD:])
            prev_ref[bi, 0:1, :] = zr[T - 1:T]
            prev_ref[bi, 1:2, :] = zk[T - 1:T]
            prev_ref[bi, 2:3, :] = zv[T - 1:T]
            prevl_ref[bi, 0:1, 0:LORA_PAD] = zwd[T - 1:T]
            prevl_ref[bi, 0:1, LORA_PAD:2 * LORA_PAD] = zad[T - 1:T]
            prevl_ref[bi, 0:1, 2 * LORA_PAD:] = zgd[T - 1:T]

            w_log = -jax.nn.softplus(-(w0 + _dot(jnp.tanh(wd), wdu_ref[...]))) - 0.5
            lw = -jnp.exp(w_log)
            a = _sigmoid(a0 + _dot(ad, wau_ref[...]))
            g = _dot(_sigmoid(gd), wgu_ref[...])
            kk = k * k_k
            kk = kk / jnp.maximum(jnp.sqrt(head_sum(kk * kk)), 1e-12)
            kmod = k * (1.0 + (a - 1.0) * k_a)
            beta = kk * a
            bonus = head_sum(r * kmod * r_k) * v

            cums = []
            for c in range(nchunk):
                parts = _dot(tri, jnp.concatenate(_split3(lw[c * C:(c + 1) * C]), axis=1))
                cums.append(parts[:, :width] + parts[:, width:2 * width] + parts[:, 2 * width:])

            for q in range(npair):
                ln = slice(q * LANES, (q + 1) * LANES)
                for c in range(nchunk):
                    sl = slice(c * C, (c + 1) * C)
                    units.append(dict(bi=bi, q=q, c=c, sl=sl, ln=ln, lw=lw[sl, ln],
                                      cum=cums[c][:, ln], kk=kk[sl, ln], r=r[sl, ln],
                                      beta=beta[sl, ln], kmod=kmod[sl, ln], v=v[sl, ln],
                                      bonus=bonus[sl, ln], g=g[sl, ln], gn_w=gn_w[:, ln],
                                      gn_b=gn_b[:, ln]))
        return units

    def st_operands(units):
        for un in units:
            cum, lw_c = un["cum"], un["lw"]
            cum_end = cum[C - 1:C]
            e_inv = jnp.exp(-cum)
            e_rem = jnp.exp(cum_end - cum)
            un["a_t"] = stack(-un["kk"] * jnp.exp(cum - lw_c))
            un["r_t"] = stack(un["r"] * jnp.exp(cum))
            un["b_h"] = (un["beta"] * e_inv).astype(BF16)
            un["k_h"] = (un["kmod"] * e_inv).astype(BF16)
            un["b_t"] = stack(un["beta"] * e_rem)
            un["k_t"] = stack(un["kmod"] * e_rem)
            un["v_s"] = stack(un["v"])
            un["g_col"] = jnp.sum(jnp.where(eye, jnp.exp(cum_end), 0.0), axis=1, keepdims=True)
        for un in units:
            un["scores"] = _dot_nt(jnp.concatenate([un["a_t"], un["r_t"]], axis=0),
                                   jnp.concatenate([un["b_h"], un["b_h"], un["k_h"], un["k_h"]],
                                                   axis=0))

    def st_couplings(units):
        for un in units:
            sc = un.pop("scores")
            a_ab = jnp.where(strict, sc[:2 * C, :2 * C], 0.0)
            un["a_ab"] = a_ab.astype(BF16)
            un["a_ak"] = jnp.where(strict, sc[:2 * C, 2 * C:], 0.0).astype(BF16)
            un["a_rb"] = jnp.where(incl, sc[2 * C:, :2 * C], 0.0).astype(BF16)
            un["a_rk"] = jnp.where(incl, sc[2 * C:, 2 * C:], 0.0).astype(BF16)
            un["inv"] = jnp.where(eye, 1.0, jnp.where(pair, a_ab, 0.0))
        for un in units:
            un["av"] = _dot(jnp.concatenate([un["a_ak"], un["a_rk"]], axis=0), un["v_s"])
        for un in units:
            un["sv"] = _dot_tn(un["k_t"], un["v_s"])

    def st_level(units, lvl):
        for un in units:
            un["inv_b"] = un["inv"].astype(BF16)
            un["tmp"] = _dot(jnp.where(lvl, un["a_ab"], zero_b), un["inv_b"])
        for un in units:
            un["inv"] = un["inv"] + _dot(un.pop("inv_b"), un.pop("tmp"))

    def st_solve(units):
        for un in units:
            av = un.pop("av")
            wu = _dot(un["inv"], jnp.concatenate([un["a_t"], av[:2 * C].astype(BF16)], axis=1))
            un["w"], un["u0"], un["y0"] = wu[:, :LANES].astype(BF16), wu[:, LANES:], av[2 * C:]

    def phase_b(units):
        chains = sorted({(un["bi"], un["q"]) for un in units})
        states = [s_ref[bi * npair + q] for bi, q in chains]
        for c in range(nchunk):
            cur = [un for un in units if un["c"] == c]
            us = [un["u0"] + _dot(un["w"], st) for un, st in zip(cur, states)]
            ys = [un["y0"] + _dot(jnp.concatenate([un["r_t"], un["a_rb"]], axis=1),
                                  jnp.concatenate([st.astype(BF16), u.astype(BF16)], axis=0))
                  for un, st, u in zip(cur, states, us)]
            states = [st * un["g_col"] + un["sv"] + _dot_tn(un["b_t"], u)
                      for un, st, u in zip(cur, states, us)]
            for un, y_s in zip(cur, ys):
                y = y_s[:C] + y_s[C:]
                mean = _head_sum(y, lane_lo) * (1.0 / N)
                yc = y - mean
                var = _head_sum(yc * yc, lane_lo) * (1.0 / N)
                yn = yc * lax.rsqrt(var + GN_EPS) * un["gn_w"] + un["gn_b"]
                o_ref[un["bi"], un["sl"], un["ln"]] = (
                    (yn + un["bonus"]) * un["g"]).astype(o_ref.dtype)
        for (bi, q), st in zip(chains, states):
            s_ref[bi * npair + q] = st

    units = prep_rows(range(nbatch))
    st_operands(units)
    st_couplings(units)
    for lvl in levels:
        st_level(units, lvl)
    st_solve(units)
    phase_b(units)


def _rwkv(z_r, pvec, mu_lora, wdu, wau, wgu):
    batch, seq, _ = z_r.shape
    width = wdu.shape[1]
    T = RWKV_STEP
    npair = next(n for n in (RWKV_PAIRS, 4, 2, 1) if (width // LANES) % n == 0)
    lw = npair * LANES
    ng = width // lw
    gate_w = wgu.shape[0]
    lora_w = 2 * LORA_PAD + gate_w
    lora0 = 3 * width // LORA_PAD
    return pl.pallas_call(
        _rwkv_kernel,
        grid=(ng, seq // T),
        in_specs=[
            pl.BlockSpec((batch, T, lw), lambda h, t: (0, t, h)),
            pl.BlockSpec((batch, T, lw), lambda h, t: (0, t, ng + h)),
            pl.BlockSpec((batch, T, lw), lambda h, t: (0, t, 2 * ng + h)),
            pl.BlockSpec((batch, T, LORA_PAD), lambda h, t: (0, t, lora0)),
            pl.BlockSpec((batch, T, LORA_PAD), lambda h, t: (0, t, lora0 + 1)),
            pl.BlockSpec((batch, T, gate_w), lambda h, t: (0, t, (lora0 + 2) * LORA_PAD // gate_w)),
            pl.BlockSpec((RWKV_PARAM_ROWS, lw), lambda h, t: (0, h)),
            pl.BlockSpec((1, lora_w), lambda h, t: (0, 0)),
            pl.BlockSpec((LORA_PAD, lw), lambda h, t: (0, h)),
            pl.BlockSpec((LORA_PAD, lw), lambda h, t: (0, h)),
            pl.BlockSpec((gate_w, lw), lambda h, t: (0, h)),
        ],
        out_specs=pl.BlockSpec((batch, T, lw), lambda h, t: (0, t, h)),
        out_shape=jax.ShapeDtypeStruct((batch, seq, width), BF16),
        scratch_shapes=[pltpu.VMEM((batch * npair, LANES, LANES), F32),
                        pltpu.VMEM((batch, SUBLANES, lw), F32),
                        pltpu.VMEM((batch, SUBLANES, lora_w), F32)],
        compiler_params=_cparams(("arbitrary", "arbitrary")),
        name="rwkv7_mix",
    )(z_r, z_r, z_r, z_r, z_r, z_r, pvec, mu_lora, wdu, wau, wgu)


def _moba_kernel(jj_ref, tt_ref, q_ref, k_ref, v_ref, slope_ref, o_ref, ka_ref, vt_ref, qa_ref,
                 kmean_ref, ma_ref, mb_ref, acca_ref, accb_ref, s_ref, *, nb, tile, ptile, ntrips):
    BLK, DH = MOBA_BLOCK, MOBA_HEAD_DIM
    nbp = kmean_ref.shape[0]
    neg = NEG_INF * LOG2E

    def rows(n):
        return pl.ds(pl.multiple_of(n * BLK, BLK), BLK)

    kmean_ref[...] = jnp.zeros_like(kmean_ref)
    lane = lax.broadcasted_iota(jnp.int32, (BLK, DH), 1)
    key_off = lax.broadcasted_iota(jnp.int32, (BLK, DH), 0).astype(F32)

    def prep(it, carry):
        blocks = [it * ptile + r for r in range(ptile)]
        kfs = [k_ref[rows(n), :] for n in blocks]
        vts = [v_ref[rows(n), :].T for n in blocks]
        for n, kf, vt in zip(blocks, kfs, vts):
            ext = jnp.where(lane < nbp, jnp.where(lane == n, 1.0, 0.0),
                            jnp.where(lane < nbp + SPLIT, lax.convert_element_type(n, F32),
                                      jnp.where(lane < nbp + 2 * SPLIT, key_off, 0.0)))
            ka_ref[n] = jnp.concatenate([kf, ext], axis=1).astype(BF16)
            vt_ref[n] = jnp.concatenate([vt, jnp.ones((MOBA_ONES, BLK), F32)],
                                        axis=0).astype(BF16)
            kmean_ref[pl.ds(n, 1), :] = jnp.mean(kf, axis=0, keepdims=True)
        return carry

    lax.fori_loop(0, nb // ptile, prep, 0)

    slope2 = slope_ref[...] * LOG2E
    coef = _split3(slope2 * float(BLK)) + _split3(slope2)
    erow = lax.broadcasted_iota(jnp.int32, (DH - nbp, BLK), 0)
    ext_q = jnp.zeros((DH - nbp, BLK), F32)
    for idx, cf in enumerate(coef):
        ext_q = jnp.where(erow == idx, cf, ext_q)
    blk_id = lax.broadcasted_iota(jnp.int32, (nbp, BLK), 0)
    k1, k2, k3 = _split3(kmean_ref[...])
    kmean6 = jnp.concatenate([k1, k1, k2, k1, k2, k3], axis=1)
    causal = (lax.broadcasted_iota(jnp.int32, (BLK, BLK), 1)
              >= lax.broadcasted_iota(jnp.int32, (BLK, BLK), 0))
    lane2 = lax.broadcasted_iota(jnp.int32, (BLK, 2 * DH), 1)
    not_onehot = (lane2 < DH) | (lane2 >= DH + nbp)

    def select(gate, i):
        past = blk_id < i
        gm = jnp.where(past, gate, NEG_INF)
        chosen = jnp.zeros((nbp, BLK), dtype=jnp.bool_)
        for _ in range(MOBA_TOPK):
            mx = jnp.max(gm, axis=0, keepdims=True)
            first = jnp.min(jnp.where(gm == mx, blk_id, nbp), axis=0, keepdims=True)
            pick = blk_id == first
            chosen = chosen | pick
            gm = jnp.where(pick, -jnp.inf, gm)
        return jnp.where(chosen & past, 0.0, neg)

    qtile = 2 * tile if nb % (2 * tile) == 0 else tile

    def qprep(it, carry):
        blocks = [it * qtile + r for r in range(qtile)]
        q_ts = [q_ref[rows(i), :].T for i in blocks]
        gates = []
        for q_t in q_ts:
            q1, q2, q3 = _split3(q_t)
            gates.append(_dot(kmean6, jnp.concatenate([q1, q2, q1, q3, q2, q1], axis=0)))
        selbs = [select(g, i) for g, i in zip(gates, blocks)]
        qas = [jnp.concatenate([q_t * (DH ** -0.5 * LOG2E), sb, ext_q], axis=0).astype(BF16)
               for q_t, sb in zip(q_ts, selbs)]
        for i, qa in zip(blocks, qas):
            qa_ref[i] = qa
        keys = [[i - r % tile + jr for jr in range(r % tile)] for r, i in enumerate(blocks)]
        own = [jnp.where(causal, _dot(jnp.where(not_onehot, ka_ref[i], 0), qa), neg)
               for i, qa in zip(blocks, qas)]
        past = [[_dot(ka_ref[j], qa) for j in js] for js, qa in zip(keys, qas)]
        ms = []
        for s, sp in zip(own, past):
            m = jnp.max(s, axis=0, keepdims=True)
            for t in sp:
                m = jnp.maximum(m, jnp.max(t, axis=0, keepdims=True))
            ms.append(m)
        for i, js, m, s, sp in zip(blocks, keys, ms, own, past):
            acc = _dot(vt_ref[i], jnp.exp2(s - m).astype(BF16))
            for j, t in zip(js, sp):
                acc = acc + _dot(vt_ref[j], jnp.exp2(t - m).astype(BF16))
            ma_ref[i] = m
            acca_ref[i] = acc
            mb_ref[i] = jnp.full((1, BLK), -jnp.inf, F32)
            accb_ref[i] = jnp.zeros(accb_ref.shape[1:], F32)
        return carry

    lax.fori_loop(0, nb // qtile, qprep, 0)

    def logits_to(slot, tr):
        kaj = ka_ref[jj_ref[tr]]
        t0 = tt_ref[tr] * tile
        mxs = []
        for r in range(tile):
            s = _dot(kaj, qa_ref[t0 + r])
            s_ref[slot, r] = s
            mxs.append(jnp.max(s, axis=0, keepdims=True))
        return tuple(mxs)

    def half_trip(slot, tr, mxs, tr_next):
        m_ref, acc_ref = (ma_ref, acca_ref) if slot == 0 else (mb_ref, accb_ref)
        mxs_next = logits_to(1 - slot, tr_next)
        vtj = vt_ref[jj_ref[tr]]
        t0 = tt_ref[tr] * tile
        for r in range(tile):
            i = t0 + r
            m = m_ref[i]
            m_new = jnp.maximum(m, mxs[r])
            p = jnp.exp2(s_ref[slot, r] - m_new)
            m_ref[i] = m_new
            acc_ref[i] = acc_ref[i] * jnp.exp2(m - m_new) + _dot(vtj, p.astype(BF16))
        return mxs_next

    def body(pi, mxs):
        tr = MOBA_UNROLL * pi
        for u in range(MOBA_UNROLL - 1):
            mxs = half_trip(u % 2, tr + u, mxs, tr + u + 1)
        last = tr + MOBA_UNROLL - 1
        return half_trip((MOBA_UNROLL - 1) % 2, last, mxs, jnp.minimum(last + 1, ntrips - 1))

    if ntrips:
        lax.fori_loop(0, ntrips // MOBA_UNROLL, body, logits_to(0, 0))

    def finish(it, carry):
        blocks = [it * ptile + r for r in range(ptile)]
        outs = []
        for i in blocks:
            ma, mb = ma_ref[i], mb_ref[i]
            m = jnp.maximum(ma, mb)
            acc = acca_ref[i] * jnp.exp2(ma - m) + accb_ref[i] * jnp.exp2(mb - m)
            outs.append((acc[:DH] / acc[DH:DH + 1]).T)
        for i, o in zip(blocks, outs):
            o_ref[rows(i), :] = o.astype(o_ref.dtype)
        return carry

    lax.fori_loop(0, nb // ptile, finish, 0)


def _moba_schedule(nb, tile):
    trips = [(j, t) for j in range(nb) for t in range(j // tile + 1, nb // tile)]
    while len(trips) % MOBA_UNROLL:
        trips.append((nb - 1, nb // tile - 1))
    return trips


def _moba(z_m, slopes, *, batch, seq):
    m, cols = z_m.shape
    heads = cols // (3 * MOBA_HEAD_DIM)
    BLK, DH = MOBA_BLOCK, MOBA_HEAD_DIM
    nb = seq // BLK
    nbp = -(-nb // SUBLANES) * SUBLANES
    assert nbp + 2 * SPLIT <= DH
    tile = next(g for g in (MOBA_GROUP, 2, 1) if nb % g == 0)
    ptile = next(g for g in (MOBA_PREP, 2, 1) if nb % g == 0)
    trips = _moba_schedule(nb, tile)
    sched = jnp.asarray(trips or [(0, 0)], jnp.int32)
    grid_spec = pltpu.PrefetchScalarGridSpec(
        num_scalar_prefetch=2,
        grid=(batch, heads),
        in_specs=[
            pl.BlockSpec((seq, DH), lambda b, h, jj, tt: (b, h)),
            pl.BlockSpec((seq, DH), lambda b, h, jj, tt: (b, heads + h)),
            pl.BlockSpec((seq, DH), lambda b, h, jj, tt: (b, 2 * heads + h)),
            pl.BlockSpec((None, 1, BLK), lambda b, h, jj, tt: (h, 0, 0)),
        ],
        out_specs=pl.BlockSpec((seq, DH), lambda b, h, jj, tt: (b, h)),
        scratch_shapes=[pltpu.VMEM((nb, BLK, 2 * DH), BF16),
                        pltpu.VMEM((nb, DH + MOBA_ONES, BLK), BF16),
                        pltpu.VMEM((nb, 2 * DH, BLK), BF16),
                        pltpu.VMEM((nbp, DH), F32),
                        pltpu.VMEM((nb, 1, BLK), F32),
                        pltpu.VMEM((nb, 1, BLK), F32),
                        pltpu.VMEM((nb, DH + MOBA_ONES, BLK), F32),
                        pltpu.VMEM((nb, DH + MOBA_ONES, BLK), F32),
                        pltpu.VMEM((2, tile, BLK, BLK), F32)])
    return pl.pallas_call(
        functools.partial(_moba_kernel, nb=nb, tile=tile, ptile=ptile, ntrips=len(trips)),
        grid_spec=grid_spec,
        out_shape=jax.ShapeDtypeStruct((m, heads * DH), BF16),
        compiler_params=_cparams(("arbitrary", "arbitrary")),
        name="moba_attn",
    )(sched[:, 0], sched[:, 1], z_m, z_m, z_m, slopes)


def _resident(shape):
    return pl.BlockSpec(shape, lambda i: (0,) * len(shape), pipeline_mode=pl.Buffered(1))


def _merge_out_kernel(x_ref, yr_ref, ym_ref, g_ref, wr_ref, wm_ref, wo_ref, o_ref):
    d = o_ref.shape[1]
    br = _dot(yr_ref[...], wr_ref[...])
    bm = _dot(ym_ref[...], wm_ref[...])
    merged = (_sigmoid(g_ref[:, :d].astype(F32)) * br
              + _sigmoid(g_ref[:, d:].astype(F32)) * bm)
    o_ref[...] = x_ref[...] + _dot(merged, wo_ref[...])


def _merge_out(x2, y_r, y_m, gates, w_r, w_m, w_o, *, tm):
    m, d = x2.shape
    kr, km = y_r.shape[1], y_m.shape[1]
    return pl.pallas_call(
        _merge_out_kernel,
        grid=(m // tm,),
        in_specs=[pl.BlockSpec((tm, d), lambda i: (i, 0)),
                  pl.BlockSpec((tm, kr), lambda i: (i, 0)),
                  pl.BlockSpec((tm, km), lambda i: (i, 0)),
                  pl.BlockSpec((tm, 2 * d), lambda i: (i, 0)),
                  _resident((kr, d)), _resident((km, d)), _resident((d, d))],
        out_specs=pl.BlockSpec((tm, d), lambda i: (i, 0)),
        out_shape=jax.ShapeDtypeStruct((m, d), F32),
        compiler_params=_cparams(("arbitrary",)),
        name="merge_out_proj",
    )(x2, y_r, y_m, gates, w_r, w_m, w_o)


def _mlp_kernel(x_ref, g_ref, w1_ref, w2_ref, o_ref, h_ref):
    @pl.when(pl.program_id(1) == 0)
    def _():
        xf = x_ref[...]
        h_ref[...] = _rms(xf, g_ref[...]).astype(BF16)
        o_ref[...] = xf

    u = jnp.maximum(_dot(h_ref[...], w1_ref[...]), 0.0)
    o_ref[...] += _dot((u * u).astype(BF16), w2_ref[...])


def _mlp(x2, g, w1, w2, *, tm):
    m, d = x2.shape
    nf, _, tf = w1.shape
    return pl.pallas_call(
        _mlp_kernel,
        grid=(m // tm, nf),
        in_specs=[pl.BlockSpec((tm, d), lambda i, j: (i, 0)),
                  pl.BlockSpec((1, d), lambda i, j: (0, 0)),
                  pl.BlockSpec((None, d, tf), lambda i, j: (j, 0, 0)),
                  pl.BlockSpec((tf, d), lambda i, j: (j, 0))],
        out_specs=pl.BlockSpec((tm, d), lambda i, j: (i, 0)),
        out_shape=jax.ShapeDtypeStruct((m, d), F32),
        scratch_shapes=[pltpu.VMEM((tm, d), BF16)],
        compiler_params=_cparams(("arbitrary", "arbitrary")),
        name="mlp_relu2",
    )(x2, g, w1, w2)


def _ple_kernel(x_ref, p_ref, g_ref, gf_ref, wg_ref, wu_ref, o_ref, *, final_norm):
    xf = x_ref[...]
    hp = _rms(xf, g_ref[...]).astype(BF16)
    gate = _sigmoid(_dot(hp, wg_ref[...]))
    up = _dot(p_ref[...].astype(BF16), wu_ref[...])
    y = xf + gate * up
    o_ref[...] = _rms(y, gf_ref[...]) if final_norm else y


def _ple(x2, p2, g, gf, wg, wu, *, tm, final_norm):
    m, d = x2.shape
    pd = p2.shape[1]
    return pl.pallas_call(
        functools.partial(_ple_kernel, final_norm=final_norm),
        grid=(m // tm,),
        in_specs=[pl.BlockSpec((tm, d), lambda i: (i, 0)),
                  pl.BlockSpec((tm, pd), lambda i: (i, 0)),
                  pl.BlockSpec((1, d), lambda i: (0, 0)),
                  pl.BlockSpec((1, d), lambda i: (0, 0)),
                  _resident((d, d)), _resident((pd, d))],
        out_specs=pl.BlockSpec((tm, d), lambda i: (i, 0)),
        out_shape=jax.ShapeDtypeStruct((m, d), F32),
        compiler_params=_cparams(("arbitrary",)),
        name="ple_final",
    )(x2, p2, g, gf, wg, wu)


def _tile_plan(m):
    def rows(cap):
        return next(t for t in (2048, 1024, 512, 256, 128, 64, 32, 16, 8)
                    if t <= cap and m % t == 0)

    return dict(tn=512, tm_norm=rows(512), tm_in=rows(2048), tm_out=rows(512), tm_mlp=rows(1024),
                tm_ple=rows(512))


def _pad_to(a, n, axis):
    pad = [(0, 0)] * a.ndim
    pad[axis] = (0, n - a.shape[axis])
    return jnp.pad(a, pad)


def kernel(x, p, g_mix, w_in, mu_shift, w0, w_decay_up, a0, w_aaa_up, w_gate_up, k_k, k_a, r_k,
           gn_w, gn_b, w_branch_rwkv, w_branch_moba, w_out, g_mlp, w_mlp_in, w_mlp_out, g_ple,
           w_ple_gate, w_ple_up, g_final):
    B, S, D = x.shape
    depth = w_in.shape[0]
    W = w0.shape[1]
    Wm = w_branch_moba.shape[1]
    dl, al, gl = w_decay_up.shape[1], w_aaa_up.shape[1], w_gate_up.shape[1]
    assert S % MOBA_BLOCK == 0 and S % RWKV_STEP == 0
    assert W % LANES == 0 and dl <= LORA_PAD and al <= LORA_PAD and gl % LANES == 0
    M = B * S
    tiles = _tile_plan(M)
    x2 = x.reshape(M, D)
    moba_heads = Wm // MOBA_HEAD_DIM
    slopes = 2.0 ** (-8.0 * (jnp.arange(moba_heads, dtype=F32) + 1.0) / moba_heads)
    slopes = jnp.broadcast_to(slopes[:, None, None], (moba_heads, 1, MOBA_BLOCK))

    for i in range(depth):
        wi = w_in[i]
        c0 = 3 * W
        c1, c2, c3 = c0 + dl, c0 + dl + al, c0 + dl + al + gl
        tn = tiles["tn"]
        w_all = _col_tiles(jnp.concatenate([wi[:, :c0], _pad_to(wi[:, c0:c1], LORA_PAD, 1),
                                            _pad_to(wi[:, c1:c2], LORA_PAD, 1), wi[:, c2:]],
                                           axis=1), tn)
        assert (c0 + 2 * LORA_PAD + gl) % tn == 0 and (3 * Wm) % tn == 0
        nt_r = (c0 + 2 * LORA_PAD + gl) // tn
        nt_m = 3 * Wm // tn
        nt_g = w_all.shape[0] - nt_r - nt_m
        mu = mu_shift[i]
        mu_lora = jnp.concatenate([_pad_to(mu[c0:c1], LORA_PAD, 0), _pad_to(mu[c1:c2], LORA_PAD, 0),
                                   mu[c2:c3]])[None, :]
        pvec = jnp.stack([w0[i], a0[i], k_k[i], k_a[i], r_k[i].reshape(-1), gn_w[i], gn_b[i],
                          mu[:W], mu[W:2 * W], mu[2 * W:3 * W]])
        pvec = _pad_to(pvec, RWKV_PARAM_ROWS, 0)
        wdu = _pad_to(w_decay_up[i], LORA_PAD, 0)
        wau = _pad_to(w_aaa_up[i], LORA_PAD, 0)
        gmix = g_mix[i][None, :]

        h = _norm(x2, gmix, tm=tiles["tm_norm"])
        z_r = _inproj(h, w_all, 0, nt_r, act="none", out_dtype=F32, tm=tiles["tm_in"])
        z_m = _inproj(h, w_all, nt_r, nt_m, act="none", out_dtype=F32, tm=tiles["tm_in"])
        gates = _inproj(h, w_all, nt_r + nt_m, nt_g, act="none", out_dtype=BF16,
                        tm=tiles["tm_in"])

        y_r = _rwkv(z_r.reshape(B, S, -1), pvec, mu_lora, wdu, wau, w_gate_up[i]).reshape(M, W)
        y_m = _moba(z_m, slopes, batch=B, seq=S)

        x2 = _merge_out(x2, y_r, y_m, gates, w_branch_rwkv[i].astype(BF16),
                        w_branch_moba[i].astype(BF16), w_out[i].astype(BF16), tm=tiles["tm_out"])
        x2 = _mlp(x2, g_mlp[i][None, :], _col_tiles(w_mlp_in[i], tn), w_mlp_out[i].astype(BF16),
                  tm=tiles["tm_mlp"])
        x2 = _ple(x2, p[i].reshape(M, -1), g_ple[i][None, :], g_final[None, :],
                  w_ple_gate[i].astype(BF16), w_ple_up[i].astype(BF16), tm=tiles["tm_ple"],
                  final_norm=(i == depth - 1))
    return x2.reshape(B, S, D)
```

```python
import functools
import math

import jax
import jax.numpy as jnp
from jax import lax
from jax.experimental import pallas as pl
from jax.experimental.pallas import tpu as pltpu

F32 = jnp.float32
BF16 = jnp.bfloat16

NORM_EPS = 1e-6
GN_EPS = 64e-5
NEG_INF = -1e30
LOG2E = math.log2(math.e)

LANES = 128
SUBLANES = 8
RWKV_HEAD_DIM = 64
RWKV_CHUNK = 64
RWKV_STEP = 128
RWKV_PAIRS = 4
MOBA_HEAD_DIM = 128
MOBA_BLOCK = 256
MOBA_TOPK = 3
MOBA_GROUP = 4
MOBA_PREP = 8
MOBA_ONES = 16
MOBA_UNROLL = 16
LORA_PAD = 128
RWKV_PARAM_ROWS = 16
SPLIT = 3
VMEM_LIMIT = 56 * 1024 * 1024


def _cparams(sem):
    return pltpu.CompilerParams(dimension_semantics=sem, vmem_limit_bytes=VMEM_LIMIT)


def _rms(xf, g):
    return xf * lax.rsqrt(jnp.mean(xf * xf, axis=-1, keepdims=True) + NORM_EPS) * g


def _sigmoid(x):
    return 0.5 * jnp.tanh(0.5 * x) + 0.5


def _dot(a, b):
    return jnp.dot(a.astype(BF16), b.astype(BF16), preferred_element_type=F32)


def _split3(c):
    hi = c.astype(BF16).astype(F32)
    mid = (c - hi).astype(BF16).astype(F32)
    lo = (c - hi - mid).astype(BF16).astype(F32)
    return hi, mid, lo


def _dot_nt(a, b):
    return lax.dot_general(a.astype(BF16), b.astype(BF16), (((1,), (1,)), ((), ())),
                           preferred_element_type=F32)


def _dot_tn(a, b):
    return lax.dot_general(a.astype(BF16), b.astype(BF16), (((0,), (0,)), ((), ())),
                           preferred_element_type=F32)


def _norm_kernel(x_ref, g_ref, o_ref):
    o_ref[...] = _rms(x_ref[...], g_ref[...]).astype(o_ref.dtype)


def _norm(x2, g, *, tm):
    m, d = x2.shape
    return pl.pallas_call(
        _norm_kernel,
        grid=(m // tm,),
        in_specs=[pl.BlockSpec((tm, d), lambda i: (i, 0)),
                  pl.BlockSpec((1, d), lambda i: (0, 0))],
        out_specs=pl.BlockSpec((tm, d), lambda i: (i, 0)),
        out_shape=jax.ShapeDtypeStruct((m, d), BF16),
        compiler_params=_cparams(("arbitrary",)),
        name="mix_norm",
    )(x2, g)


def _inproj_kernel(h_ref, w_ref, o_ref, *, act):
    z = _dot(h_ref[...], w_ref[...])
    if act == "sigmoid":
        z = _sigmoid(z)
    o_ref[...] = z.astype(o_ref.dtype)


def _col_tiles(w, tn):
    k, n = w.shape
    return w.astype(BF16).reshape(k, n // tn, tn).transpose(1, 0, 2)


def _inproj(h, w, *, act, out_dtype, tm):
    m, d = h.shape
    nt, _, tn = w.shape
    n = nt * tn
    return pl.pallas_call(
        functools.partial(_inproj_kernel, act=act),
        grid=(m // tm, nt),
        in_specs=[pl.BlockSpec((tm, d), lambda i, j: (i, 0)),
                  pl.BlockSpec((None, d, tn), lambda i, j: (j, 0, 0))],
        out_specs=pl.BlockSpec((tm, tn), lambda i, j: (i, j)),
        out_shape=jax.ShapeDtypeStruct((m, n), out_dtype),
        compiler_params=_cparams(("arbitrary", "arbitrary")),
        name="inproj_" + act,
    )(h, w)


def _head_sum(x, lane_lo):
    s0 = jnp.sum(jnp.where(lane_lo, x, 0.0), axis=-1, keepdims=True)
    s1 = jnp.sum(jnp.where(lane_lo, 0.0, x), axis=-1, keepdims=True)
    return jnp.where(lane_lo, s0, s1)


def _rwkv_kernel(r_ref, k_ref, v_ref, wd_ref, ad_ref, gd_ref, pv_ref, mul_ref,
                 wdu_ref, wau_ref, wgu_ref, o_ref, s_ref, prev_ref, prevl_ref):
    C, T, N = RWKV_CHUNK, RWKV_STEP, RWKV_HEAD_DIM
    nbatch = r_ref.shape[0]
    width = r_ref.shape[2]
    npair = width // LANES
    nchunk = T // C

    @pl.when(pl.program_id(1) == 0)
    def _():
        s_ref[...] = jnp.zeros_like(s_ref)
        prev_ref[...] = jnp.zeros_like(prev_ref)
        prevl_ref[...] = jnp.zeros_like(prevl_ref)

    row_t = lax.broadcasted_iota(jnp.int32, (T, 1), 0)

    def shift(z, prev_row, mu):
        zp = jnp.where(row_t == 0, prev_row, pltpu.roll(z, 1, 0))
        return z + mu * (zp - z)

    pv = pv_ref[...]
    w0, a0, k_k, k_a, r_k, gn_w, gn_b = (pv[i:i + 1] for i in range(7))
    mu_r, mu_k, mu_v = pv[7:8], pv[8:9], pv[9:10]
    mul = mul_ref[...]
    lane_lo = lax.broadcasted_iota(jnp.int32, (1, LANES), 1) < N

    def head_sum(x):
        return jnp.concatenate([_head_sum(x[:, q * LANES:(q + 1) * LANES], lane_lo)
                                for q in range(npair)], axis=1)

    row = lax.broadcasted_iota(jnp.int32, (2 * C, 2 * C), 0)
    col = lax.broadcasted_iota(jnp.int32, (2 * C, 2 * C), 1)
    same_head = (row // C) == (col // C)
    strict = same_head & (row > col)
    incl = same_head & (row >= col)
    eye = row == col
    pair = (row // 2) == (col // 2)
    levels = []
    b = 2
    while b < C:
        levels.append(((row // (2 * b)) == (col // (2 * b))) & ((row // b) != (col // b)))
        b *= 2
    tri = (lax.broadcasted_iota(jnp.int32, (C, C), 0)
           >= lax.broadcasted_iota(jnp.int32, (C, C), 1)).astype(F32)

    zero_b = jnp.zeros((), BF16)

    def stack(x):
        xb = x.astype(BF16)
        return jnp.concatenate([jnp.where(lane_lo, xb, zero_b), jnp.where(lane_lo, zero_b, xb)],
                               axis=0)

    def prep_rows(rows_):
        units = []
        for bi in rows_:
            zr, zk, zv = r_ref[bi], k_ref[bi], v_ref[bi]
            zwd, zad, zgd = wd_ref[bi], ad_ref[bi], gd_ref[bi]
            pr = prev_ref[bi]
            pl_ = prevl_ref[bi]
            r = shift(zr, pr[0:1], mu_r)
            k = shift(zk, pr[1:2], mu_k)
            v = shift(zv, pr[2:3], mu_v)
            wd = shift(zwd, pl_[0:1, 0:LORA_PAD], mul[:, 0:LORA_PAD])
            ad = shift(zad, pl_[0:1, LORA_PAD:2 * LORA_PAD], mul[:, LORA_PAD:2 * LORA_PAD])
            gd = shift(zgd, pl_[0:1, 2 * LORA_PAD:], mul[:, 2 * LORA_PAD:])
            prev_ref[bi, 0:1, :] = zr[T - 1:T]
            prev_ref[bi, 1:2, :] = zk[T - 1:T]
            prev_ref[bi, 2:3, :] = zv[T - 1:T]
            prevl_ref[bi, 0:1, 0:LORA_PAD] = zwd[T - 1:T]
            prevl_ref[bi, 0:1, LORA_PAD:2 * LORA_PAD] = zad[T - 1:T]
            prevl_ref[bi, 0:1, 2 * LORA_PAD:] = zgd[T - 1:T]

            w_log = -jax.nn.softplus(-(w0 + _dot(jnp.tanh(wd), wdu_ref[...]))) - 0.5
            lw = -jnp.exp(w_log)
            a = _sigmoid(a0 + _dot(ad, wau_ref[...]))
            g = _dot(_sigmoid(gd), wgu_ref[...])
            kk = k * k_k
            kk = kk / jnp.maximum(jnp.sqrt(head_sum(kk * kk)), 1e-12)
            kmod = k * (1.0 + (a - 1.0) * k_a)
            beta = kk * a
            bonus = head_sum(r * kmod * r_k) * v

            cums = []
            for c in range(nchunk):
                parts = _dot(tri, jnp.concatenate(_split3(lw[c * C:(c + 1) * C]), axis=1))
                cums.append(parts[:, :width] + parts[:, width:2 * width] + parts[:, 2 * width:])

            for q in range(npair):
                ln = slice(q * LANES, (q + 1) * LANES)
                for c in range(nchunk):
                    sl = slice(c * C, (c + 1) * C)
                    units.append(dict(bi=bi, q=q, c=c, sl=sl, ln=ln, lw=lw[sl, ln],
                                      cum=cums[c][:, ln], kk=kk[sl, ln], r=r[sl, ln],
                                      beta=beta[sl, ln], kmod=kmod[sl, ln], v=v[sl, ln],
                                      bonus=bonus[sl, ln], g=g[sl, ln], gn_w=gn_w[:, ln],
                                      gn_b=gn_b[:, ln]))
        return units

    def st_operands(units):
        for un in units:
            cum, lw_c = un["cum"], un["lw"]
            cum_end = cum[C - 1:C]
            e_inv = jnp.exp(-cum)
            e_rem = jnp.exp(cum_end - cum)
            un["a_t"] = stack(-un["kk"] * jnp.exp(cum - lw_c))
            un["r_t"] = stack(un["r"] * jnp.exp(cum))
            un["b_h"] = (un["beta"] * e_inv).astype(BF16)
            un["k_h"] = (un["kmod"] * e_inv).astype(BF16)
            un["b_t"] = stack(un["beta"] * e_rem)
            un["k_t"] = stack(un["kmod"] * e_rem)
            un["v_s"] = stack(un["v"])
            un["g_col"] = jnp.sum(jnp.where(eye, jnp.exp(cum_end), 0.0), axis=1, keepdims=True)
        for un in units:
            un["scores"] = _dot_nt(jnp.concatenate([un["a_t"], un["r_t"]], axis=0),
                                   jnp.concatenate([un["b_h"], un["b_h"], un["k_h"], un["k_h"]],
                                                   axis=0))

    def st_couplings(units):
        for un in units:
            sc = un.pop("scores")
            a_ab = jnp.where(strict, sc[:2 * C, :2 * C], 0.0)
            un["a_ab"] = a_ab.astype(BF16)
            un["a_ak"] = jnp.where(strict, sc[:2 * C, 2 * C:], 0.0).astype(BF16)
            un["a_rb"] = jnp.where(incl, sc[2 * C:, :2 * C], 0.0).astype(BF16)
            un["a_rk"] = jnp.where(incl, sc[2 * C:, 2 * C:], 0.0).astype(BF16)
            un["inv"] = jnp.where(eye, 1.0, jnp.where(pair, a_ab, 0.0))
        for un in units:
            un["av"] = _dot(jnp.concatenate([un["a_ak"], un["a_rk"]], axis=0), un["v_s"])
        for un in units:
            un["sv"] = _dot_tn(un["k_t"], un["v_s"])

    def st_level(units, lvl):
        for un in units:
            un["inv_b"] = un["inv"].astype(BF16)
            un["tmp"] = _dot(jnp.where(lvl, un["a_ab"], zero_b), un["inv_b"])
        for un in units:
            un["inv"] = un["inv"] + _dot(un.pop("inv_b"), un.pop("tmp"))

    def st_solve(units):
        for un in units:
            av = un.pop("av")
            wu = _dot(un["inv"], jnp.concatenate([un["a_t"], av[:2 * C].astype(BF16)], axis=1))
            un["w"], un["u0"], un["y0"] = wu[:, :LANES].astype(BF16), wu[:, LANES:], av[2 * C:]

    def phase_b(units):
        chains = sorted({(un["bi"], un["q"]) for un in units})
        states = [s_ref[bi * npair + q] for bi, q in chains]
        for c in range(nchunk):
            cur = [un for un in units if un["c"] == c]
            us = [un["u0"] + _dot(un["w"], st) for un, st in zip(cur, states)]
            ys = [un["y0"] + _dot(jnp.concatenate([un["r_t"], un["a_rb"]], axis=1),
                                  jnp.concatenate([st.astype(BF16), u.astype(BF16)], axis=0))
                  for un, st, u in zip(cur, states, us)]
            states = [st * un["g_col"] + un["sv"] + _dot_tn(un["b_t"], u)
                      for un, st, u in zip(cur, states, us)]
            for un, y_s in zip(cur, ys):
                y = y_s[:C] + y_s[C:]
                mean = _head_sum(y, lane_lo) * (1.0 / N)
                yc = y - mean
                var = _head_sum(yc * yc, lane_lo) * (1.0 / N)
                yn = yc * lax.rsqrt(var + GN_EPS) * un["gn_w"] + un["gn_b"]
                o_ref[un["bi"], un["sl"], un["ln"]] = (
                    (yn + un["bonus"]) * un["g"]).astype(o_ref.dtype)
        for (bi, q), st in zip(chains, states):
            s_ref[bi * npair + q] = st

    units = prep_rows(range(nbatch))
    st_operands(units)
    st_couplings(units)
    for lvl in levels:
        st_level(units, lvl)
    st_solve(units)
    phase_b(units)


def _rwkv(z_r, pvec, mu_lora, wdu, wau, wgu):
    batch, seq, _ = z_r.shape
    width = wdu.shape[1]
    T = RWKV_STEP
    npair = next(n for n in (RWKV_PAIRS, 4, 2, 1) if (width // LANES) % n == 0)
    lw = npair * LANES
    ng = width // lw
    gate_w = wgu.shape[0]
    lora_w = 2 * LORA_PAD + gate_w
    lora0 = 3 * width // LORA_PAD
    return pl.pallas_call(
        _rwkv_kernel,
        grid=(ng, seq // T),
        in_specs=[
            pl.BlockSpec((batch, T, lw), lambda h, t: (0, t, h)),
            pl.BlockSpec((batch, T, lw), lambda h, t: (0, t, ng + h)),
            pl.BlockSpec((batch, T, lw), lambda h, t: (0, t, 2 * ng + h)),
            pl.BlockSpec((batch, T, LORA_PAD), lambda h, t: (0, t, lora0)),
            pl.BlockSpec((batch, T, LORA_PAD), lambda h, t: (0, t, lora0 + 1)),
            pl.BlockSpec((batch, T, gate_w), lambda h, t: (0, t, (lora0 + 2) * LORA_PAD // gate_w)),
            pl.BlockSpec((RWKV_PARAM_ROWS, lw), lambda h, t: (0, h)),
            pl.BlockSpec((1, lora_w), lambda h, t: (0, 0)),
            pl.BlockSpec((LORA_PAD, lw), lambda h, t: (0, h)),
            pl.BlockSpec((LORA_PAD, lw), lambda h, t: (0, h)),
            pl.BlockSpec((gate_w, lw), lambda h, t: (0, h)),
        ],
        out_specs=pl.BlockSpec((batch, T, lw), lambda h, t: (0, t, h)),
        out_shape=jax.ShapeDtypeStruct((batch, seq, width), BF16),
        scratch_shapes=[pltpu.VMEM((batch * npair, LANES, LANES), F32),
                        pltpu.VMEM((batch, SUBLANES, lw), F32),
                        pltpu.VMEM((batch, SUBLANES, lora_w), F32)],
        compiler_params=_cparams(("arbitrary", "arbitrary")),
        name="rwkv7_mix",
    )(z_r, z_r, z_r, z_r, z_r, z_r, pvec, mu_lora, wdu, wau, wgu)


def _moba_kernel(jj_ref, tt_ref, q_ref, k_ref, v_ref, slope_ref, o_ref, ka_ref, vt_ref, qa_ref,
                 kmean_ref, ma_ref, mb_ref, acca_ref, accb_ref, s_ref, *, nb, tile, ptile, ntrips):
    BLK, DH = MOBA_BLOCK, MOBA_HEAD_DIM
    nbp = kmean_ref.shape[0]
    neg = NEG_INF * LOG2E

    def rows(n):
        return pl.ds(pl.multiple_of(n * BLK, BLK), BLK)

    kmean_ref[...] = jnp.zeros_like(kmean_ref)
    lane = lax.broadcasted_iota(jnp.int32, (BLK, DH), 1)
    key_off = lax.broadcasted_iota(jnp.int32, (BLK, DH), 0).astype(F32)

    def prep(it, carry):
        blocks = [it * ptile + r for r in range(ptile)]
        kfs = [k_ref[rows(n), :] for n in blocks]
        vts = [v_ref[rows(n), :].T for n in blocks]
        for n, kf, vt in zip(blocks, kfs, vts):
            ext = jnp.where(lane < nbp, jnp.where(lane == n, 1.0, 0.0),
                            jnp.where(lane < nbp + SPLIT, lax.convert_element_type(n, F32),
                                      jnp.where(lane < nbp + 2 * SPLIT, key_off, 0.0)))
            ka_ref[n] = jnp.concatenate([kf, ext], axis=1).astype(BF16)
            vt_ref[n] = jnp.concatenate([vt, jnp.ones((MOBA_ONES, BLK), F32)],
                                        axis=0).astype(BF16)
            kmean_ref[pl.ds(n, 1), :] = jnp.mean(kf, axis=0, keepdims=True)
        return carry

    lax.fori_loop(0, nb // ptile, prep, 0)

    slope2 = slope_ref[...] * LOG2E
    coef = _split3(slope2 * float(BLK)) + _split3(slope2)
    erow = lax.broadcasted_iota(jnp.int32, (DH - nbp, BLK), 0)
    ext_q = jnp.zeros((DH - nbp, BLK), F32)
    for idx, cf in enumerate(coef):
        ext_q = jnp.where(erow == idx, cf, ext_q)
    blk_id = lax.broadcasted_iota(jnp.int32, (nbp, BLK), 0)
    k1, k2, k3 = _split3(kmean_ref[...])
    kmean6 = jnp.concatenate([k1, k1, k2, k1, k2, k3], axis=1)
    causal = (lax.broadcasted_iota(jnp.int32, (BLK, BLK), 1)
              >= lax.broadcasted_iota(jnp.int32, (BLK, BLK), 0))
    lane2 = lax.broadcasted_iota(jnp.int32, (BLK, 2 * DH), 1)
    not_onehot = (lane2 < DH) | (lane2 >= DH + nbp)

    def select(gate, i):
        past = blk_id < i
        gm = jnp.where(past, gate, NEG_INF)
        chosen = jnp.zeros((nbp, BLK), dtype=jnp.bool_)
        for _ in range(MOBA_TOPK):
            mx = jnp.max(gm, axis=0, keepdims=True)
            first = jnp.min(jnp.where(gm == mx, blk_id, nbp), axis=0, keepdims=True)
            pick = blk_id == first
            chosen = chosen | pick
            gm = jnp.where(pick, -jnp.inf, gm)
        return jnp.where(chosen & past, 0.0, neg)

    qtile = 2 * tile if nb % (2 * tile) == 0 else tile

    def qprep(it, carry):
        blocks = [it * qtile + r for r in range(qtile)]
        q_ts = [q_ref[rows(i), :].T for i in blocks]
        gates = []
        for q_t in q_ts:
            q1, q2, q3 = _split3(q_t)
            gates.append(_dot(kmean6, jnp.concatenate([q1, q2, q1, q3, q2, q1], axis=0)))
        selbs = [select(g, i) for g, i in zip(gates, blocks)]
        qas = [jnp.concatenate([q_t * (DH ** -0.5 * LOG2E), sb, ext_q], axis=0).astype(BF16)
               for q_t, sb in zip(q_ts, selbs)]
        for i, qa in zip(blocks, qas):
            qa_ref[i] = qa
        keys = [[i - r % tile + jr for jr in range(r % tile)] for r, i in enumerate(blocks)]
        own = [jnp.where(causal, _dot(jnp.where(not_onehot, ka_ref[i], 0), qa), neg)
               for i, qa in zip(blocks, qas)]
        past = [[_dot(ka_ref[j], qa) for j in js] for js, qa in zip(keys, qas)]
        ms = []
        for s, sp in zip(own, past):
            m = jnp.max(s, axis=0, keepdims=True)
            for t in sp:
                m = jnp.maximum(m, jnp.max(t, axis=0, keepdims=True))
            ms.append(m)
        for i, js, m, s, sp in zip(blocks, keys, ms, own, past):
            acc = _dot(vt_ref[i], jnp.exp2(s - m).astype(BF16))
            for j, t in zip(js, sp):
                acc = acc + _dot(vt_ref[j], jnp.exp2(t - m).astype(BF16))
            ma_ref[i] = m
            acca_ref[i] = acc
            mb_ref[i] = jnp.full((1, BLK), -jnp.inf, F32)
            accb_ref[i] = jnp.zeros(accb_ref.shape[1:], F32)
        return carry

    lax.fori_loop(0, nb // qtile, qprep, 0)

    def logits_to(slot, tr):
        kaj = ka_ref[jj_ref[tr]]
        t0 = tt_ref[tr] * tile
        mxs = []
        for r in range(tile):
            s = _dot(kaj, qa_ref[t0 + r])
            s_ref[slot, r] = s
            mxs.append(jnp.max(s, axis=0, keepdims=True))
        return tuple(mxs)

    def half_trip(slot, tr, mxs, tr_next):
        m_ref, acc_ref = (ma_ref, acca_ref) if slot == 0 else (mb_ref, accb_ref)
        mxs_next = logits_to(1 - slot, tr_next)
        vtj = vt_ref[jj_ref[tr]]
        t0 = tt_ref[tr] * tile
        for r in range(tile):
            i = t0 + r
            m = m_ref[i]
            m_new = jnp.maximum(m, mxs[r])
            p = jnp.exp2(s_ref[slot, r] - m_new)
            m_ref[i] = m_new
            acc_ref[i] = acc_ref[i] * jnp.exp2(m - m_new) + _dot(vtj, p.astype(BF16))
        return mxs_next

    def body(pi, mxs):
        tr = MOBA_UNROLL * pi
        for u in range(MOBA_UNROLL - 1):
            mxs = half_trip(u % 2, tr + u, mxs, tr + u + 1)
        last = tr + MOBA_UNROLL - 1
        return half_trip((MOBA_UNROLL - 1) % 2, last, mxs, jnp.minimum(last + 1, ntrips - 1))

    if ntrips:
        lax.fori_loop(0, ntrips // MOBA_UNROLL, body, logits_to(0, 0))

    def finish(it, carry):
        blocks = [it * ptile + r for r in range(ptile)]
        outs = []
        for i in blocks:
            ma, mb = ma_ref[i], mb_ref[i]
            m = jnp.maximum(ma, mb)
            acc = acca_ref[i] * jnp.exp2(ma - m) + accb_ref[i] * jnp.exp2(mb - m)
            outs.append((acc[:DH] / acc[DH:DH + 1]).T)
        for i, o in zip(blocks, outs):
            o_ref[rows(i), :] = o.astype(o_ref.dtype)
        return carry

    lax.fori_loop(0, nb // ptile, finish, 0)


def _moba_schedule(nb, tile):
    trips = [(j, t) for j in range(nb) for t in range(j // tile + 1, nb // tile)]
    while len(trips) % MOBA_UNROLL:
        trips.append((nb - 1, nb // tile - 1))
    return trips


def _moba(z_m, slopes, *, batch, seq):
    m, cols = z_m.shape
    heads = cols // (3 * MOBA_HEAD_DIM)
    BLK, DH = MOBA_BLOCK, MOBA_HEAD_DIM
    nb = seq // BLK
    nbp = -(-nb // SUBLANES) * SUBLANES
    assert nbp + 2 * SPLIT <= DH
    tile = next(g for g in (MOBA_GROUP, 2, 1) if nb % g == 0)
    ptile = next(g for g in (MOBA_PREP, 4, 2, 1) if nb % g == 0)
    trips = _moba_schedule(nb, tile)
    sched = jnp.asarray(trips or [(0, 0)], jnp.int32)
    grid_spec = pltpu.PrefetchScalarGridSpec(
        num_scalar_prefetch=2,
        grid=(batch, heads),
        in_specs=[
            pl.BlockSpec((seq, DH), lambda b, h, jj, tt: (b, h)),
            pl.BlockSpec((seq, DH), lambda b, h, jj, tt: (b, heads + h)),
            pl.BlockSpec((seq, DH), lambda b, h, jj, tt: (b, 2 * heads + h)),
            pl.BlockSpec((None, 1, BLK), lambda b, h, jj, tt: (h, 0, 0)),
        ],
        out_specs=pl.BlockSpec((seq, DH), lambda b, h, jj, tt: (b, h)),
        scratch_shapes=[pltpu.VMEM((nb, BLK, 2 * DH), BF16),
                        pltpu.VMEM((nb, DH + MOBA_ONES, BLK), BF16),
                        pltpu.VMEM((nb, 2 * DH, BLK), BF16),
                        pltpu.VMEM((nbp, DH), F32),
                        pltpu.VMEM((nb, 1, BLK), F32),
                        pltpu.VMEM((nb, 1, BLK), F32),
                        pltpu.VMEM((nb, DH + MOBA_ONES, BLK), F32),
                        pltpu.VMEM((nb, DH + MOBA_ONES, BLK), F32),
                        pltpu.VMEM((2, tile, BLK, BLK), F32)])
    return pl.pallas_call(
        functools.partial(_moba_kernel, nb=nb, tile=tile, ptile=ptile, ntrips=len(trips)),
        grid_spec=grid_spec,
        out_shape=jax.ShapeDtypeStruct((m, heads * DH), BF16),
        compiler_params=_cparams(("arbitrary", "arbitrary")),
        name="moba_attn",
    )(sched[:, 0], sched[:, 1], z_m, z_m, z_m, slopes)


def _resident(shape):
    return pl.BlockSpec(shape, lambda i: (0,) * len(shape), pipeline_mode=pl.Buffered(1))


def _merge_out_kernel(x_ref, yr_ref, ym_ref, g_ref, wr_ref, wm_ref, wo_ref, o_ref):
    d = o_ref.shape[1]
    br = _dot(yr_ref[...], wr_ref[...])
    bm = _dot(ym_ref[...], wm_ref[...])
    merged = (_sigmoid(g_ref[:, :d].astype(F32)) * br
              + _sigmoid(g_ref[:, d:].astype(F32)) * bm)
    o_ref[...] = x_ref[...] + _dot(merged, wo_ref[...])


def _merge_out(x2, y_r, y_m, gates, w_r, w_m, w_o, *, tm):
    m, d = x2.shape
    kr, km = y_r.shape[1], y_m.shape[1]
    return pl.pallas_call(
        _merge_out_kernel,
        grid=(m // tm,),
        in_specs=[pl.BlockSpec((tm, d), lambda i: (i, 0)),
                  pl.BlockSpec((tm, kr), lambda i: (i, 0)),
                  pl.BlockSpec((tm, km), lambda i: (i, 0)),
                  pl.BlockSpec((tm, 2 * d), lambda i: (i, 0)),
                  _resident((kr, d)), _resident((km, d)), _resident((d, d))],
        out_specs=pl.BlockSpec((tm, d), lambda i: (i, 0)),
        out_shape=jax.ShapeDtypeStruct((m, d), F32),
        compiler_params=_cparams(("arbitrary",)),
        name="merge_out_proj",
    )(x2, y_r, y_m, gates, w_r, w_m, w_o)


def _mlp_kernel(x_ref, g_ref, w1_ref, w2_ref, o_ref, h_ref):
    @pl.when(pl.program_id(1) == 0)
    def _():
        xf = x_ref[...]
        h_ref[...] = _rms(xf, g_ref[...]).astype(BF16)
        o_ref[...] = xf

    u = jnp.maximum(_dot(h_ref[...], w1_ref[...]), 0.0)
    o_ref[...] += _dot((u * u).astype(BF16), w2_ref[...])


def _mlp(x2, g, w1, w2, *, tm):
    m, d = x2.shape
    nf, _, tf = w1.shape
    return pl.pallas_call(
        _mlp_kernel,
        grid=(m // tm, nf),
        in_specs=[pl.BlockSpec((tm, d), lambda i, j: (i, 0)),
                  pl.BlockSpec((1, d), lambda i, j: (0, 0)),
                  pl.BlockSpec((None, d, tf), lambda i, j: (j, 0, 0)),
                  pl.BlockSpec((tf, d), lambda i, j: (j, 0))],
        out_specs=pl.BlockSpec((tm, d), lambda i, j: (i, 0)),
        out_shape=jax.ShapeDtypeStruct((m, d), F32),
        scratch_shapes=[pltpu.VMEM((tm, d), BF16)],
        compiler_params=_cparams(("arbitrary", "arbitrary")),
        name="mlp_relu2",
    )(x2, g, w1, w2)


def _ple_kernel(x_ref, p_ref, g_ref, gf_ref, wg_ref, wu_ref, o_ref, *, final_norm):
    xf = x_ref[...]
    hp = _rms(xf, g_ref[...]).astype(BF16)
    gate = _sigmoid(_dot(hp, wg_ref[...]))
    up = _dot(p_ref[...].astype(BF16), wu_ref[...])
    y = xf + gate * up
    o_ref[...] = _rms(y, gf_ref[...]) if final_norm else y


def _ple(x2, p2, g, gf, wg, wu, *, tm, final_norm):
    m, d = x2.shape
    pd = p2.shape[1]
    return pl.pallas_call(
        functools.partial(_ple_kernel, final_norm=final_norm),
        grid=(m // tm,),
        in_specs=[pl.BlockSpec((tm, d), lambda i: (i, 0)),
                  pl.BlockSpec((tm, pd), lambda i: (i, 0)),
                  pl.BlockSpec((1, d), lambda i: (0, 0)),
                  pl.BlockSpec((1, d), lambda i: (0, 0)),
                  _resident((d, d)), _resident((pd, d))],
        out_specs=pl.BlockSpec((tm, d), lambda i: (i, 0)),
        out_shape=jax.ShapeDtypeStruct((m, d), F32),
        compiler_params=_cparams(("arbitrary",)),
        name="ple_final",
    )(x2, p2, g, gf, wg, wu)


def _tile_plan(m):
    def rows(cap):
        return next(t for t in (2048, 1024, 512, 256, 128, 64, 32, 16, 8)
                    if t <= cap and m % t == 0)

    return dict(tn=512, tm_norm=rows(512), tm_in=rows(2048), tm_out=rows(512), tm_mlp=rows(1024),
                tm_ple=rows(512))


def _pad_to(a, n, axis):
    pad = [(0, 0)] * a.ndim
    pad[axis] = (0, n - a.shape[axis])
    return jnp.pad(a, pad)


def kernel(x, p, g_mix, w_in, mu_shift, w0, w_decay_up, a0, w_aaa_up, w_gate_up, k_k, k_a, r_k,
           gn_w, gn_b, w_branch_rwkv, w_branch_moba, w_out, g_mlp, w_mlp_in, w_mlp_out, g_ple,
           w_ple_gate, w_ple_up, g_final):
    B, S, D = x.shape
    depth = w_in.shape[0]
    W = w0.shape[1]
    Wm = w_branch_moba.shape[1]
    dl, al, gl = w_decay_up.shape[1], w_aaa_up.shape[1], w_gate_up.shape[1]
    assert S % MOBA_BLOCK == 0 and S % RWKV_STEP == 0
    assert W % LANES == 0 and dl <= LORA_PAD and al <= LORA_PAD and gl % LANES == 0
    M = B * S
    tiles = _tile_plan(M)
    x2 = x.reshape(M, D)
    moba_heads = Wm // MOBA_HEAD_DIM
    slopes = 2.0 ** (-8.0 * (jnp.arange(moba_heads, dtype=F32) + 1.0) / moba_heads)
    slopes = jnp.broadcast_to(slopes[:, None, None], (moba_heads, 1, MOBA_BLOCK))

    for i in range(depth):
        wi = w_in[i]
        c0 = 3 * W
        c1, c2, c3 = c0 + dl, c0 + dl + al, c0 + dl + al + gl
        tn = tiles["tn"]
        w_r = _col_tiles(jnp.concatenate([wi[:, :c0], _pad_to(wi[:, c0:c1], LORA_PAD, 1),
                                          _pad_to(wi[:, c1:c2], LORA_PAD, 1), wi[:, c2:c3]],
                                         axis=1), tn)
        w_m = _col_tiles(wi[:, c3:c3 + 3 * Wm], tn)
        w_g = _col_tiles(wi[:, c3 + 3 * Wm:], tn)
        mu = mu_shift[i]
        mu_lora = jnp.concatenate([_pad_to(mu[c0:c1], LORA_PAD, 0), _pad_to(mu[c1:c2], LORA_PAD, 0),
                                   mu[c2:c3]])[None, :]
        pvec = jnp.stack([w0[i], a0[i], k_k[i], k_a[i], r_k[i].reshape(-1), gn_w[i], gn_b[i],
                          mu[:W], mu[W:2 * W], mu[2 * W:3 * W]])
        pvec = _pad_to(pvec, RWKV_PARAM_ROWS, 0)
        wdu = _pad_to(w_decay_up[i], LORA_PAD, 0)
        wau = _pad_to(w_aaa_up[i], LORA_PAD, 0)
        gmix = g_mix[i][None, :]

        h = _norm(x2, gmix, tm=tiles["tm_norm"])
        z_r = _inproj(h, w_r, act="none", out_dtype=F32, tm=tiles["tm_in"])
        z_m = _inproj(h, w_m, act="none", out_dtype=F32, tm=tiles["tm_in"])
        gates = _inproj(h, w_g, act="none", out_dtype=BF16, tm=tiles["tm_in"])

        y_r = _rwkv(z_r.reshape(B, S, -1), pvec, mu_lora, wdu, wau, w_gate_up[i]).reshape(M, W)
        y_m = _moba(z_m, slopes, batch=B, seq=S)

        x2 = _merge_out(x2, y_r, y_m, gates, w_branch_rwkv[i].astype(BF16),
                        w_branch_moba[i].astype(BF16), w_out[i].astype(BF16), tm=tiles["tm_out"])
        x2 = _mlp(x2, g_mlp[i][None, :], _col_tiles(w_mlp_in[i], tn), w_mlp_out[i].astype(BF16),
                  tm=tiles["tm_mlp"])
        x2 = _ple(x2, p[i].reshape(M, -1), g_ple[i][None, :], g_final[None, :],
                  w_ple_gate[i].astype(BF16), w_ple_up[i].astype(BF16), tm=tiles["tm_ple"],
                  final_norm=(i == depth - 1))
    return x2.reshape(B, S, D)
```

```python
import functools
import math

import jax
import jax.numpy as jnp
from jax import lax
from jax.experimental import pallas as pl
from jax.experimental.pallas import tpu as pltpu

F32 = jnp.float32
BF16 = jnp.bfloat16

NORM_EPS = 1e-6
GN_EPS = 64e-5
NEG_INF = -1e30
LOG2E = math.log2(math.e)

LANES = 128
SUBLANES = 8
RWKV_HEAD_DIM = 64
RWKV_CHUNK = 64
RWKV_STEP = 128
RWKV_PAIRS = 4
MOBA_HEAD_DIM = 128
MOBA_BLOCK = 256
MOBA_TOPK = 3
MOBA_GROUP = 4
MOBA_PREP = 8
MOBA_ONES = 16
MOBA_UNROLL = 16
LORA_PAD = 128
RWKV_PARAM_ROWS = 16
SPLIT = 3
VMEM_LIMIT = 56 * 1024 * 1024


def _cparams(sem):
    return pltpu.CompilerParams(dimension_semantics=sem, vmem_limit_bytes=VMEM_LIMIT)


def _rms(xf, g):
    return xf * lax.rsqrt(jnp.mean(xf * xf, axis=-1, keepdims=True) + NORM_EPS) * g


def _sigmoid(x):
    return 0.5 * jnp.tanh(0.5 * x) + 0.5


def _dot(a, b):
    return jnp.dot(a.astype(BF16), b.astype(BF16), preferred_element_type=F32)


def _split3(c):
    hi = c.astype(BF16).astype(F32)
    mid = (c - hi).astype(BF16).astype(F32)
    lo = (c - hi - mid).astype(BF16).astype(F32)
    return hi, mid, lo


def _dot_nt(a, b):
    return lax.dot_general(a.astype(BF16), b.astype(BF16), (((1,), (1,)), ((), ())),
                           preferred_element_type=F32)


def _dot_tn(a, b):
    return lax.dot_general(a.astype(BF16), b.astype(BF16), (((0,), (0,)), ((), ())),
                           preferred_element_type=F32)


def _norm_kernel(x_ref, g_ref, o_ref):
    o_ref[...] = _rms(x_ref[...], g_ref[...]).astype(o_ref.dtype)


def _norm(x2, g, *, tm):
    m, d = x2.shape
    return pl.pallas_call(
        _norm_kernel,
        grid=(m // tm,),
        in_specs=[pl.BlockSpec((tm, d), lambda i: (i, 0)),
                  pl.BlockSpec((1, d), lambda i: (0, 0))],
        out_specs=pl.BlockSpec((tm, d), lambda i: (i, 0)),
        out_shape=jax.ShapeDtypeStruct((m, d), BF16),
        compiler_params=_cparams(("arbitrary",)),
        name="mix_norm",
    )(x2, g)


def _inproj_kernel(h_ref, w_ref, o_ref, *, act):
    z = _dot(h_ref[...], w_ref[...])
    if act == "sigmoid":
        z = _sigmoid(z)
    o_ref[...] = z.astype(o_ref.dtype)


def _col_tiles(w, tn):
    k, n = w.shape
    return w.astype(BF16).reshape(k, n // tn, tn).transpose(1, 0, 2)


def _inproj(h, w, *, act, out_dtype, tm):
    m, d = h.shape
    nt, _, tn = w.shape
    n = nt * tn
    return pl.pallas_call(
        functools.partial(_inproj_kernel, act=act),
        grid=(m // tm, nt),
        in_specs=[pl.BlockSpec((tm, d), lambda i, j: (i, 0)),
                  pl.BlockSpec((None, d, tn), lambda i, j: (j, 0, 0))],
        out_specs=pl.BlockSpec((tm, tn), lambda i, j: (i, j)),
        out_shape=jax.ShapeDtypeStruct((m, n), out_dtype),
        compiler_params=_cparams(("arbitrary", "arbitrary")),
        name="inproj_" + act,
    )(h, w)


def _head_sum(x, lane_lo):
    s0 = jnp.sum(jnp.where(lane_lo, x, 0.0), axis=-1, keepdims=True)
    s1 = jnp.sum(jnp.where(lane_lo, 0.0, x), axis=-1, keepdims=True)
    return jnp.where(lane_lo, s0, s1)


def _rwkv_kernel(r_ref, k_ref, v_ref, wd_ref, ad_ref, gd_ref, pv_ref, mul_ref,
                 wdu_ref, wau_ref, wgu_ref, o_ref, s_ref, prev_ref, prevl_ref):
    C, T, N = RWKV_CHUNK, RWKV_STEP, RWKV_HEAD_DIM
    nbatch = r_ref.shape[0]
    width = r_ref.shape[2]
    npair = width // LANES
    nchunk = T // C

    @pl.when(pl.program_id(1) == 0)
    def _():
        s_ref[...] = jnp.zeros_like(s_ref)
        prev_ref[...] = jnp.zeros_like(prev_ref)
        prevl_ref[...] = jnp.zeros_like(prevl_ref)

    row_t = lax.broadcasted_iota(jnp.int32, (T, 1), 0)

    def shift(z, prev_row, mu):
        zp = jnp.where(row_t == 0, prev_row, pltpu.roll(z, 1, 0))
        return z + mu * (zp - z)

    pv = pv_ref[...]
    w0, a0, k_k, k_a, r_k, gn_w, gn_b = (pv[i:i + 1] for i in range(7))
    mu_r, mu_k, mu_v = pv[7:8], pv[8:9], pv[9:10]
    mul = mul_ref[...]
    lane_lo = lax.broadcasted_iota(jnp.int32, (1, LANES), 1) < N

    def head_sum(x):
        return jnp.concatenate([_head_sum(x[:, q * LANES:(q + 1) * LANES], lane_lo)
                                for q in range(npair)], axis=1)

    row = lax.broadcasted_iota(jnp.int32, (2 * C, 2 * C), 0)
    col = lax.broadcasted_iota(jnp.int32, (2 * C, 2 * C), 1)
    same_head = (row // C) == (col // C)
    strict = same_head & (row > col)
    incl = same_head & (row >= col)
    eye = row == col
    pair = (row // 2) == (col // 2)
    levels = []
    b = 2
    while b < C:
        levels.append(((row // (2 * b)) == (col // (2 * b))) & ((row // b) != (col // b)))
        b *= 2
    tri = (lax.broadcasted_iota(jnp.int32, (C, C), 0)
           >= lax.broadcasted_iota(jnp.int32, (C, C), 1)).astype(F32)

    zero_b = jnp.zeros((), BF16)

    def stack(x):
        xb = x.astype(BF16)
        return jnp.concatenate([jnp.where(lane_lo, xb, zero_b), jnp.where(lane_lo, zero_b, xb)],
                               axis=0)

    def prep_rows(rows_):
        units = []
        for bi in rows_:
            zr, zk, zv = r_ref[bi], k_ref[bi], v_ref[bi]
            zwd, zad, zgd = wd_ref[bi], ad_ref[bi], gd_ref[bi]
            pr = prev_ref[bi]
            pl_ = prevl_ref[bi]
            r = shift(zr, pr[0:1], mu_r)
            k = shift(zk, pr[1:2], mu_k)
            v = shift(zv, pr[2:3], mu_v)
            wd = shift(zwd, pl_[0:1, 0:LORA_PAD], mul[:, 0:LORA_PAD])
            ad = shift(zad, pl_[0:1, LORA_PAD:2 * LORA_PAD], mul[:, LORA_PAD:2 * LORA_PAD])
            gd = shift(zgd, pl_[0:1, 2 * LORA_PAD:], mul[:, 2 * LORA_PAD:])
            prev_ref[bi, 0:1, :] = zr[T - 1:T]
            prev_ref[bi, 1:2, :] = zk[T - 1:T]
            prev_ref[bi, 2:3, :] = zv[T - 1:T]
            prevl_ref[bi, 0:1, 0:LORA_PAD] = zwd[T - 1:T]
            prevl_ref[bi, 0:1, LORA_PAD:2 * LORA_PAD] = zad[T - 1:T]
            prevl_ref[bi, 0:1, 2 * LORA_PAD:] = zgd[T - 1:T]

            w_log = -jax.nn.softplus(-(w0 + _dot(jnp.tanh(wd), wdu_ref[...]))) - 0.5
            lw = -jnp.exp(w_log)
            a = _sigmoid(a0 + _dot(ad, wau_ref[...]))
            g = _dot(_sigmoid(gd), wgu_ref[...])
            kk = k * k_k
            kk = kk / jnp.maximum(jnp.sqrt(head_sum(kk * kk)), 1e-12)
            kmod = k * (1.0 + (a - 1.0) * k_a)
            beta = kk * a
            bonus = head_sum(r * kmod * r_k) * v

            cums = []
            for c in range(nchunk):
                parts = _dot(tri, jnp.concatenate(_split3(lw[c * C:(c + 1) * C]), axis=1))
                cums.append(parts[:, :width] + parts[:, width:2 * width] + parts[:, 2 * width:])

            for q in range(npair):
                ln = slice(q * LANES, (q + 1) * LANES)
                for c in range(nchunk):
                    sl = slice(c * C, (c + 1) * C)
                    units.append(dict(bi=bi, q=q, c=c, sl=sl, ln=ln, lw=lw[sl, ln],
                                      cum=cums[c][:, ln], kk=kk[sl, ln], r=r[sl, ln],
                                      beta=beta[sl, ln], kmod=kmod[sl, ln], v=v[sl, ln],
                                      bonus=bonus[sl, ln], g=g[sl, ln], gn_w=gn_w[:, ln],
                                      gn_b=gn_b[:, ln]))
        return units

    def st_operands(units):
        for un in units:
            cum, lw_c = un["cum"], un["lw"]
            cum_end = cum[C - 1:C]
            e_inv = jnp.exp(-cum)
            e_rem = jnp.exp(cum_end - cum)
            un["a_t"] = stack(-un["kk"] * jnp.exp(cum - lw_c))
            un["r_t"] = stack(un["r"] * jnp.exp(cum))
            un["b_h"] = (un["beta"] * e_inv).astype(BF16)
            un["k_h"] = (un["kmod"] * e_inv).astype(BF16)
            un["b_t"] = stack(un["beta"] * e_rem)
            un["k_t"] = stack(un["kmod"] * e_rem)
            un["v_s"] = stack(un["v"])
            un["g_col"] = jnp.sum(jnp.where(eye, jnp.exp(cum_end), 0.0), axis=1, keepdims=True)
        for un in units:
            un["scores"] = _dot_nt(jnp.concatenate([un["a_t"], un["r_t"]], axis=0),
                                   jnp.concatenate([un["b_h"], un["b_h"], un["k_h"], un["k_h"]],
                                                   axis=0))

    def st_couplings(units):
        for un in units:
            sc = un.pop("scores")
            a_ab = jnp.where(strict, sc[:2 * C, :2 * C], 0.0)
            un["a_ab"] = a_ab.astype(BF16)
            un["a_ak"] = jnp.where(strict, sc[:2 * C, 2 * C:], 0.0).astype(BF16)
            un["a_rb"] = jnp.where(incl, sc[2 * C:, :2 * C], 0.0).astype(BF16)
            un["a_rk"] = jnp.where(incl, sc[2 * C:, 2 * C:], 0.0).astype(BF16)
            un["inv"] = jnp.where(eye, 1.0, jnp.where(pair, a_ab, 0.0))
        for un in units:
            un["av"] = _dot(jnp.concatenate([un["a_ak"], un["a_rk"]], axis=0), un["v_s"])
        for un in units:
            un["sv"] = _dot_tn(un["k_t"], un["v_s"])

    def st_level(units, lvl):
        for un in units:
            un["inv_b"] = un["inv"].astype(BF16)
            un["tmp"] = _dot(jnp.where(lvl, un["a_ab"], zero_b), un["inv_b"])
        for un in units:
            un["inv"] = un["inv"] + _dot(un.pop("inv_b"), un.pop("tmp"))

    def st_solve(units):
        for un in units:
            av = un.pop("av")
            wu = _dot(un["inv"], jnp.concatenate([un["a_t"], av[:2 * C].astype(BF16)], axis=1))
            un["w"], un["u0"], un["y0"] = wu[:, :LANES].astype(BF16), wu[:, LANES:], av[2 * C:]

    def phase_b(units):
        chains = sorted({(un["bi"], un["q"]) for un in units})
        states = [s_ref[bi * npair + q] for bi, q in chains]
        for c in range(nchunk):
            cur = [un for un in units if un["c"] == c]
            us = [un["u0"] + _dot(un["w"], st) for un, st in zip(cur, states)]
            ys = [un["y0"] + _dot(jnp.concatenate([un["r_t"], un["a_rb"]], axis=1),
                                  jnp.concatenate([st.astype(BF16), u.astype(BF16)], axis=0))
                  for un, st, u in zip(cur, states, us)]
            states = [st * un["g_col"] + un["sv"] + _dot_tn(un["b_t"], u)
                      for un, st, u in zip(cur, states, us)]
            for un, y_s in zip(cur, ys):
                y = y_s[:C] + y_s[C:]
                mean = _head_sum(y, lane_lo) * (1.0 / N)
                yc = y - mean
                var = _head_sum(yc * yc, lane_lo) * (1.0 / N)
                yn = yc * lax.rsqrt(var + GN_EPS) * un["gn_w"] + un["gn_b"]
                o_ref[un["bi"], un["sl"], un["ln"]] = (
                    (yn + un["bonus"]) * un["g"]).astype(o_ref.dtype)
        for (bi, q), st in zip(chains, states):
            s_ref[bi * npair + q] = st

    units = prep_rows(range(nbatch))
    st_operands(units)
    st_couplings(units)
    for lvl in levels:
        st_level(units, lvl)
    st_solve(units)
    phase_b(units)


def _rwkv(z_r, pvec, mu_lora, wdu, wau, wgu):
    batch, seq, _ = z_r.shape
    width = wdu.shape[1]
    T = RWKV_STEP
    npair = next(n for n in (RWKV_PAIRS, 4, 2, 1) if (width // LANES) % n == 0)
    lw = npair * LANES
    ng = width // lw
    gate_w = wgu.shape[0]
    lora_w = 2 * LORA_PAD + gate_w
    lora0 = 3 * width // LORA_PAD
    return pl.pallas_call(
        _rwkv_kernel,
        grid=(ng, seq // T),
        in_specs=[
            pl.BlockSpec((batch, T, lw), lambda h, t: (0, t, h)),
            pl.BlockSpec((batch, T, lw), lambda h, t: (0, t, ng + h)),
            pl.BlockSpec((batch, T, lw), lambda h, t: (0, t, 2 * ng + h)),
            pl.BlockSpec((batch, T, LORA_PAD), lambda h, t: (0, t, lora0)),
            pl.BlockSpec((batch, T, LORA_PAD), lambda h, t: (0, t, lora0 + 1)),
            pl.BlockSpec((batch, T, gate_w), lambda h, t: (0, t, (lora0 + 2) * LORA_PAD // gate_w)),
            pl.BlockSpec((RWKV_PARAM_ROWS, lw), lambda h, t: (0, h)),
            pl.BlockSpec((1, lora_w), lambda h, t: (0, 0)),
            pl.BlockSpec((LORA_PAD, lw), lambda h, t: (0, h)),
            pl.BlockSpec((LORA_PAD, lw), lambda h, t: (0, h)),
            pl.BlockSpec((gate_w, lw), lambda h, t: (0, h)),
        ],
        out_specs=pl.BlockSpec((batch, T, lw), lambda h, t: (0, t, h)),
        out_shape=jax.ShapeDtypeStruct((batch, seq, width), BF16),
        scratch_shapes=[pltpu.VMEM((batch * npair, LANES, LANES), F32),
                        pltpu.VMEM((batch, SUBLANES, lw), F32),
                        pltpu.VMEM((batch, SUBLANES, lora_w), F32)],
        compiler_params=_cparams(("arbitrary", "arbitrary")),
        name="rwkv7_mix",
    )(z_r, z_r, z_r, z_r, z_r, z_r, pvec, mu_lora, wdu, wau, wgu)


def _moba_kernel(jj_ref, tt_ref, q_ref, k_ref, v_ref, slope_ref, o_ref, ka_ref, vt_ref, qa_ref,
                 kmean_ref, ma_ref, mb_ref, acca_ref, accb_ref, s_ref, *, nb, tile, ptile, ntrips):
    BLK, DH = MOBA_BLOCK, MOBA_HEAD_DIM
    nbp = kmean_ref.shape[0]
    neg = NEG_INF * LOG2E

    def rows(n):
        return pl.ds(pl.multiple_of(n * BLK, BLK), BLK)

    kmean_ref[...] = jnp.zeros_like(kmean_ref)
    lane = lax.broadcasted_iota(jnp.int32, (BLK, DH), 1)
    key_off = lax.broadcasted_iota(jnp.int32, (BLK, DH), 0).astype(F32)

    def prep(it, carry):
        blocks = [it * ptile + r for r in range(ptile)]
        kfs = [k_ref[rows(n), :] for n in blocks]
        vts = [v_ref[rows(n), :].T for n in blocks]
        for n, kf, vt in zip(blocks, kfs, vts):
            ext = jnp.where(lane < nbp, jnp.where(lane == n, 1.0, 0.0),
                            jnp.where(lane < nbp + SPLIT, lax.convert_element_type(n, F32),
                                      jnp.where(lane < nbp + 2 * SPLIT, key_off, 0.0)))
            ka_ref[n] = jnp.concatenate([kf, ext], axis=1).astype(BF16)
            vt_ref[n] = jnp.concatenate([vt, jnp.ones((MOBA_ONES, BLK), F32)],
                                        axis=0).astype(BF16)
            kmean_ref[pl.ds(n, 1), :] = jnp.mean(kf, axis=0, keepdims=True)
        return carry

    lax.fori_loop(0, nb // ptile, prep, 0)

    slope2 = slope_ref[...] * LOG2E
    coef = _split3(slope2 * float(BLK)) + _split3(slope2)
    erow = lax.broadcasted_iota(jnp.int32, (DH - nbp, BLK), 0)
    ext_q = jnp.zeros((DH - nbp, BLK), F32)
    for idx, cf in enumerate(coef):
        ext_q = jnp.where(erow == idx, cf, ext_q)
    blk_id = lax.broadcasted_iota(jnp.int32, (nbp, BLK), 0)
    k1, k2, k3 = _split3(kmean_ref[...])
    kmean6 = jnp.concatenate([k1, k1, k2, k1, k2, k3], axis=1)
    causal = (lax.broadcasted_iota(jnp.int32, (BLK, BLK), 1)
              >= lax.broadcasted_iota(jnp.int32, (BLK, BLK), 0))
    lane2 = lax.broadcasted_iota(jnp.int32, (BLK, 2 * DH), 1)
    not_onehot = (lane2 < DH) | (lane2 >= DH + nbp)

    def select(gate, i):
        past = blk_id < i
        gm = jnp.where(past, gate, NEG_INF)
        chosen = jnp.zeros((nbp, BLK), dtype=jnp.bool_)
        for _ in range(MOBA_TOPK):
            mx = jnp.max(gm, axis=0, keepdims=True)
            first = jnp.min(jnp.where(gm == mx, blk_id, nbp), axis=0, keepdims=True)
            pick = blk_id == first
            chosen = chosen | pick
            gm = jnp.where(pick, -jnp.inf, gm)
        return jnp.where(chosen & past, 0.0, neg)

    qtile = 2 * tile if nb % (2 * tile) == 0 else tile

    def qprep(it, carry):
        blocks = [it * qtile + r for r in range(qtile)]
        q_ts = [q_ref[rows(i), :].T for i in blocks]
        gates = []
        for q_t in q_ts:
            q1, q2, q3 = _split3(q_t)
            gates.append(_dot(kmean6, jnp.concatenate([q1, q2, q1, q3, q2, q1], axis=0)))
        selbs = [select(g, i) for g, i in zip(gates, blocks)]
        qas = [jnp.concatenate([q_t * (DH ** -0.5 * LOG2E), sb, ext_q], axis=0).astype(BF16)
               for q_t, sb in zip(q_ts, selbs)]
        for i, qa in zip(blocks, qas):
            qa_ref[i] = qa
        keys = [[i - r % tile + jr for jr in range(r % tile)] for r, i in enumerate(blocks)]
        own = [jnp.where(causal, _dot(jnp.where(not_onehot, ka_ref[i], 0), qa), neg)
               for i, qa in zip(blocks, qas)]
        past = [[_dot(ka_ref[j], qa) for j in js] for js, qa in zip(keys, qas)]
        ms = []
        for s, sp in zip(own, past):
            m = jnp.max(s, axis=0, keepdims=True)
            for t in sp:
                m = jnp.maximum(m, jnp.max(t, axis=0, keepdims=True))
            ms.append(m)
        for i, js, m, s, sp in zip(blocks, keys, ms, own, past):
            acc = _dot(vt_ref[i], jnp.exp2(s - m).astype(BF16))
            for j, t in zip(js, sp):
                acc = acc + _dot(vt_ref[j], jnp.exp2(t - m).astype(BF16))
            ma_ref[i] = m
            acca_ref[i] = acc
            mb_ref[i] = jnp.full((1, BLK), -jnp.inf, F32)
            accb_ref[i] = jnp.zeros(accb_ref.shape[1:], F32)
        return carry

    lax.fori_loop(0, nb // qtile, qprep, 0)

    def logits_to(slot, tr):
        kaj = ka_ref[jj_ref[tr]]
        t0 = tt_ref[tr] * tile
        mxs = []
        for r in range(tile):
            s = _dot(kaj, qa_ref[t0 + r])
            s_ref[slot, r] = s
            mxs.append(jnp.max(s, axis=0, keepdims=True))
        return tuple(mxs)

    def half_trip(slot, tr, mxs, tr_next):
        m_ref, acc_ref = (ma_ref, acca_ref) if slot == 0 else (mb_ref, accb_ref)
        mxs_next = logits_to(1 - slot, tr_next)
        vtj = vt_ref[jj_ref[tr]]
        t0 = tt_ref[tr] * tile
        for r in range(tile):
            i = t0 + r
            m = m_ref[i]
            m_new = jnp.maximum(m, mxs[r])
            p = jnp.exp2(s_ref[slot, r] - m_new)
            m_ref[i] = m_new
            acc_ref[i] = acc_ref[i] * jnp.exp2(m - m_new) + _dot(vtj, p.astype(BF16))
        return mxs_next

    def body(pi, mxs):
        tr = MOBA_UNROLL * pi
        for u in range(MOBA_UNROLL - 1):
            mxs = half_trip(u % 2, tr + u, mxs, tr + u + 1)
        last = tr + MOBA_UNROLL - 1
        return half_trip((MOBA_UNROLL - 1) % 2, last, mxs, jnp.minimum(last + 1, ntrips - 1))

    if ntrips:
        lax.fori_loop(0, ntrips // MOBA_UNROLL, body, logits_to(0, 0))

    def finish(it, carry):
        blocks = [it * ptile + r for r in range(ptile)]
        outs = []
        for i in blocks:
            ma, mb = ma_ref[i], mb_ref[i]
            m = jnp.maximum(ma, mb)
            acc = acca_ref[i] * jnp.exp2(ma - m) + accb_ref[i] * jnp.exp2(mb - m)
            outs.append((acc[:DH] / acc[DH:DH + 1]).T)
        for i, o in zip(blocks, outs):
            o_ref[rows(i), :] = o.astype(o_ref.dtype)
        return carry

    lax.fori_loop(0, nb // ptile, finish, 0)


def _moba_schedule(nb, tile):
    trips = [(j, t) for j in range(nb) for t in range(j // tile + 1, nb // tile)]
    while len(trips) % MOBA_UNROLL:
        trips.append((nb - 1, nb // tile - 1))
    return trips


def _moba(z_m, slopes, *, batch, seq):
    m, cols = z_m.shape
    heads = cols // (3 * MOBA_HEAD_DIM)
    BLK, DH = MOBA_BLOCK, MOBA_HEAD_DIM
    nb = seq // BLK
    nbp = -(-nb // SUBLANES) * SUBLANES
    assert nbp + 2 * SPLIT <= DH
    tile = next(g for g in (MOBA_GROUP, 2, 1) if nb % g == 0)
    ptile = next(g for g in (MOBA_PREP, 4, 2, 1) if nb % g == 0)
    trips = _moba_schedule(nb, tile)
    sched = jnp.asarray(trips or [(0, 0)], jnp.int32)
    grid_spec = pltpu.PrefetchScalarGridSpec(
        num_scalar_prefetch=2,
        grid=(batch, heads),
        in_specs=[
            pl.BlockSpec((seq, DH), lambda b, h, jj, tt: (b, h)),
            pl.BlockSpec((seq, DH), lambda b, h, jj, tt: (b, heads + h)),
            pl.BlockSpec((seq, DH), lambda b, h, jj, tt: (b, 2 * heads + h)),
            pl.BlockSpec((None, 1, BLK), lambda b, h, jj, tt: (h, 0, 0)),
        ],
        out_specs=pl.BlockSpec((seq, DH), lambda b, h, jj, tt: (b, h)),
        scratch_shapes=[pltpu.VMEM((nb, BLK, 2 * DH), BF16),
                        pltpu.VMEM((nb, DH + MOBA_ONES, BLK), BF16),
                        pltpu.VMEM((nb, 2 * DH, BLK), BF16),
                        pltpu.VMEM((nbp, DH), F32),
                        pltpu.VMEM((nb, 1, BLK), F32),
                        pltpu.VMEM((nb, 1, BLK), F32),
                        pltpu.VMEM((nb, DH + MOBA_ONES, BLK), F32),
                        pltpu.VMEM((nb, DH + MOBA_ONES, BLK), F32),
                        pltpu.VMEM((2, tile, BLK, BLK), F32)])
    return pl.pallas_call(
        functools.partial(_moba_kernel, nb=nb, tile=tile, ptile=ptile, ntrips=len(trips)),
        grid_spec=grid_spec,
        out_shape=jax.ShapeDtypeStruct((m, heads * DH), BF16),
        compiler_params=_cparams(("arbitrary", "arbitrary")),
        name="moba_attn",
    )(sched[:, 0], sched[:, 1], z_m, z_m, z_m, slopes)


def _resident(shape):
    return pl.BlockSpec(shape, lambda i: (0,) * len(shape), pipeline_mode=pl.Buffered(1))


def _merge_out_kernel(x_ref, yr_ref, ym_ref, g_ref, wr_ref, wm_ref, wo_ref, o_ref):
    d = o_ref.shape[1]
    br = _dot(yr_ref[...], wr_ref[...])
    bm = _dot(ym_ref[...], wm_ref[...])
    merged = (_sigmoid(g_ref[:, :d].astype(F32)) * br
              + _sigmoid(g_ref[:, d:].astype(F32)) * bm)
    o_ref[...] = x_ref[...] + _dot(merged, wo_ref[...])


def _merge_out(x2, y_r, y_m, gates, w_r, w_m, w_o, *, tm):
    m, d = x2.shape
    kr, km = y_r.shape[1], y_m.shape[1]
    return pl.pallas_call(
        _merge_out_kernel,
        grid=(m // tm,),
        in_specs=[pl.BlockSpec((tm, d), lambda i: (i, 0)),
                  pl.BlockSpec((tm, kr), lambda i: (i, 0)),
                  pl.BlockSpec((tm, km), lambda i: (i, 0)),
                  pl.BlockSpec((tm, 2 * d), lambda i: (i, 0)),
                  _resident((kr, d)), _resident((km, d)), _resident((d, d))],
        out_specs=pl.BlockSpec((tm, d), lambda i: (i, 0)),
        out_shape=jax.ShapeDtypeStruct((m, d), F32),
        compiler_params=_cparams(("arbitrary",)),
        name="merge_out_proj",
    )(x2, y_r, y_m, gates, w_r, w_m, w_o)


def _mlp_kernel(x_ref, g_ref, w1_ref, w2_ref, o_ref, h_ref):
    @pl.when(pl.program_id(1) == 0)
    def _():
        xf = x_ref[...]
        h_ref[...] = _rms(xf, g_ref[...]).astype(BF16)
        o_ref[...] = xf

    u = jnp.maximum(_dot(h_ref[...], w1_ref[...]), 0.0)
    o_ref[...] += _dot((u * u).astype(BF16), w2_ref[...])


def _mlp(x2, g, w1, w2, *, tm):
    m, d = x2.shape
    nf, _, tf = w1.shape
    return pl.pallas_call(
        _mlp_kernel,
        grid=(m // tm, nf),
        in_specs=[pl.BlockSpec((tm, d), lambda i, j: (i, 0)),
                  pl.BlockSpec((1, d), lambda i, j: (0, 0)),
                  pl.BlockSpec((None, d, tf), lambda i, j: (j, 0, 0)),
                  pl.BlockSpec((tf, d), lambda i, j: (j, 0))],
        out_specs=pl.BlockSpec((tm, d), lambda i, j: (i, 0)),
        out_shape=jax.ShapeDtypeStruct((m, d), F32),
        scratch_shapes=[pltpu.VMEM((tm, d), BF16)],
        compiler_params=_cparams(("arbitrary", "arbitrary")),
        name="mlp_relu2",
    )(x2, g, w1, w2)


def _ple_kernel(x_ref, p_ref, g_ref, gf_ref, wg_ref, wu_ref, o_ref, *, final_norm):
    xf = x_ref[...]
    hp = _rms(xf, g_ref[...]).astype(BF16)
    gate = _sigmoid(_dot(hp, wg_ref[...]))
    up = _dot(p_ref[...].astype(BF16), wu_ref[...])
    y = xf + gate * up
    o_ref[...] = _rms(y, gf_ref[...]) if final_norm else y


def _ple(x2, p2, g, gf, wg, wu, *, tm, final_norm):
    m, d = x2.shape
    pd = p2.shape[1]
    return pl.pallas_call(
        functools.partial(_ple_kernel, final_norm=final_norm),
        grid=(m // tm,),
        in_specs=[pl.BlockSpec((tm, d), lambda i: (i, 0)),
                  pl.BlockSpec((tm, pd), lambda i: (i, 0)),
                  pl.BlockSpec((1, d), lambda i: (0, 0)),
                  pl.BlockSpec((1, d), lambda i: (0, 0)),
                  _resident((d, d)), _resident((pd, d))],
        out_specs=pl.BlockSpec((tm, d), lambda i: (i, 0)),
        out_shape=jax.ShapeDtypeStruct((m, d), F32),
        compiler_params=_cparams(("arbitrary",)),
        name="ple_final",
    )(x2, p2, g, gf, wg, wu)


def _tile_plan(m):
    def rows(cap):
        return next(t for t in (2048, 1024, 512, 256, 128, 64, 32, 16, 8)
                    if t <= cap and m % t == 0)

    return dict(tn=512, tm_norm=rows(512), tm_in=rows(2048), tm_out=rows(512), tm_mlp=rows(1024),
                tm_ple=rows(512))


def _wide_tile(n, tn):
    return 2 * tn if n % (2 * tn) == 0 else tn


def _pad_to(a, n, axis):
    pad = [(0, 0)] * a.ndim
    pad[axis] = (0, n - a.shape[axis])
    return jnp.pad(a, pad)


def kernel(x, p, g_mix, w_in, mu_shift, w0, w_decay_up, a0, w_aaa_up, w_gate_up, k_k, k_a, r_k,
           gn_w, gn_b, w_branch_rwkv, w_branch_moba, w_out, g_mlp, w_mlp_in, w_mlp_out, g_ple,
           w_ple_gate, w_ple_up, g_final):
    B, S, D = x.shape
    depth = w_in.shape[0]
    W = w0.shape[1]
    Wm = w_branch_moba.shape[1]
    dl, al, gl = w_decay_up.shape[1], w_aaa_up.shape[1], w_gate_up.shape[1]
    assert S % MOBA_BLOCK == 0 and S % RWKV_STEP == 0
    assert W % LANES == 0 and dl <= LORA_PAD and al <= LORA_PAD and gl % LANES == 0
    M = B * S
    tiles = _tile_plan(M)
    x2 = x.reshape(M, D)
    moba_heads = Wm // MOBA_HEAD_DIM
    slopes = 2.0 ** (-8.0 * (jnp.arange(moba_heads, dtype=F32) + 1.0) / moba_heads)
    slopes = jnp.broadcast_to(slopes[:, None, None], (moba_heads, 1, MOBA_BLOCK))

    for i in range(depth):
        wi = w_in[i]
        c0 = 3 * W
        c1, c2, c3 = c0 + dl, c0 + dl + al, c0 + dl + al + gl
        tn = tiles["tn"]
        w_r = _col_tiles(jnp.concatenate([wi[:, :c0], _pad_to(wi[:, c0:c1], LORA_PAD, 1),
                                          _pad_to(wi[:, c1:c2], LORA_PAD, 1), wi[:, c2:c3]],
                                         axis=1), tn)
        w_m = _col_tiles(wi[:, c3:c3 + 3 * Wm], _wide_tile(3 * Wm, tn))
        w_g = _col_tiles(wi[:, c3 + 3 * Wm:], _wide_tile(wi.shape[1] - c3 - 3 * Wm, tn))
        mu = mu_shift[i]
        mu_lora = jnp.concatenate([_pad_to(mu[c0:c1], LORA_PAD, 0), _pad_to(mu[c1:c2], LORA_PAD, 0),
                                   mu[c2:c3]])[None, :]
        pvec = jnp.stack([w0[i], a0[i], k_k[i], k_a[i], r_k[i].reshape(-1), gn_w[i], gn_b[i],
                          mu[:W], mu[W:2 * W], mu[2 * W:3 * W]])
        pvec = _pad_to(pvec, RWKV_PARAM_ROWS, 0)
        wdu = _pad_to(w_decay_up[i], LORA_PAD, 0)
        wau = _pad_to(w_aaa_up[i], LORA_PAD, 0)
        gmix = g_mix[i][None, :]

        h = _norm(x2, gmix, tm=tiles["tm_norm"])
        z_r = _inproj(h, w_r, act="none", out_dtype=F32, tm=tiles["tm_in"])
        z_m = _inproj(h, w_m, act="none", out_dtype=F32, tm=tiles["tm_in"])
        gates = _inproj(h, w_g, act="none", out_dtype=BF16, tm=tiles["tm_in"])

        y_r = _rwkv(z_r.reshape(B, S, -1), pvec, mu_lora, wdu, wau, w_gate_up[i]).reshape(M, W)
        y_m = _moba(z_m, slopes, batch=B, seq=S)

        x2 = _merge_out(x2, y_r, y_m, gates, w_branch_rwkv[i].astype(BF16),
                        w_branch_moba[i].astype(BF16), w_out[i].astype(BF16), tm=tiles["tm_out"])
        x2 = _mlp(x2, g_mlp[i][None, :], _col_tiles(w_mlp_in[i], tn), w_mlp_out[i].astype(BF16),
                  tm=tiles["tm_mlp"])
        x2 = _ple(x2, p[i].reshape(M, -1), g_ple[i][None, :], g_final[None, :],
                  w_ple_gate[i].astype(BF16), w_ple_up[i].astype(BF16), tm=tiles["tm_ple"],
                  final_norm=(i == depth - 1))
    return x2.reshape(B, S, D)
```

```python
import functools
import math

import jax
import jax.numpy as jnp
from jax import lax
from jax.experimental import pallas as pl
from jax.experimental.pallas import tpu as pltpu

F32 = jnp.float32
BF16 = jnp.bfloat16

NORM_EPS = 1e-6
GN_EPS = 64e-5
NEG_INF = -1e30
LOG2E = math.log2(math.e)

LANES = 128
SUBLANES = 8
RWKV_HEAD_DIM = 64
RWKV_CHUNK = 64
RWKV_STEP = 128
RWKV_PAIRS = 4
MOBA_HEAD_DIM = 128
MOBA_BLOCK = 256
MOBA_TOPK = 3
MOBA_GROUP = 4
MOBA_PREP = 8
MOBA_ONES = 16
MOBA_UNROLL = 16
LORA_PAD = 128
RWKV_PARAM_ROWS = 16
SPLIT = 3
VMEM_LIMIT = 56 * 1024 * 1024


def _cparams(sem):
    return pltpu.CompilerParams(dimension_semantics=sem, vmem_limit_bytes=VMEM_LIMIT)


def _rms(xf, g):
    return xf * lax.rsqrt(jnp.mean(xf * xf, axis=-1, keepdims=True) + NORM_EPS) * g


def _sigmoid(x):
    return 0.5 * jnp.tanh(0.5 * x) + 0.5


def _dot(a, b):
    return jnp.dot(a.astype(BF16), b.astype(BF16), preferred_element_type=F32)


def _split3(c):
    hi = c.astype(BF16).astype(F32)
    mid = (c - hi).astype(BF16).astype(F32)
    lo = (c - hi - mid).astype(BF16).astype(F32)
    return hi, mid, lo


def _dot_nt(a, b):
    return lax.dot_general(a.astype(BF16), b.astype(BF16), (((1,), (1,)), ((), ())),
                           preferred_element_type=F32)


def _dot_tn(a, b):
    return lax.dot_general(a.astype(BF16), b.astype(BF16), (((0,), (0,)), ((), ())),
                           preferred_element_type=F32)


def _norm_kernel(x_ref, g_ref, o_ref):
    o_ref[...] = _rms(x_ref[...], g_ref[...]).astype(o_ref.dtype)


def _norm(x2, g, *, tm):
    m, d = x2.shape
    return pl.pallas_call(
        _norm_kernel,
        grid=(m // tm,),
        in_specs=[pl.BlockSpec((tm, d), lambda i: (i, 0)),
                  pl.BlockSpec((1, d), lambda i: (0, 0))],
        out_specs=pl.BlockSpec((tm, d), lambda i: (i, 0)),
        out_shape=jax.ShapeDtypeStruct((m, d), BF16),
        compiler_params=_cparams(("arbitrary",)),
        name="mix_norm",
    )(x2, g)


def _inproj_kernel(h_ref, w_ref, o_ref, *, act):
    z = _dot(h_ref[...], w_ref[...])
    if act == "sigmoid":
        z = _sigmoid(z)
    o_ref[...] = z.astype(o_ref.dtype)


def _norm_inproj_kernel(x_ref, g_ref, w_ref, o_ref, h_ref):
    @pl.when(pl.program_id(1) == 0)
    def _():
        h_ref[...] = _rms(x_ref[...], g_ref[...]).astype(h_ref.dtype)

    o_ref[...] = _dot(h_ref[...], w_ref[...]).astype(o_ref.dtype)


def _norm_inproj(x2, g, w, *, out_dtype, tm):
    m, d = x2.shape
    nt, _, tn = w.shape
    return pl.pallas_call(
        _norm_inproj_kernel,
        grid=(m // tm, nt),
        in_specs=[pl.BlockSpec((tm, d), lambda i, j: (i, 0)),
                  pl.BlockSpec((1, d), lambda i, j: (0, 0)),
                  pl.BlockSpec((None, d, tn), lambda i, j: (j, 0, 0))],
        out_specs=[pl.BlockSpec((tm, tn), lambda i, j: (i, j)),
                   pl.BlockSpec((tm, d), lambda i, j: (i, 0))],
        out_shape=[jax.ShapeDtypeStruct((m, nt * tn), out_dtype),
                   jax.ShapeDtypeStruct((m, d), BF16)],
        compiler_params=_cparams(("arbitrary", "arbitrary")),
        name="norm_inproj",
    )(x2, g, w)


def _col_tiles(w, tn):
    k, n = w.shape
    return w.astype(BF16).reshape(k, n // tn, tn).transpose(1, 0, 2)


def _inproj(h, w, *, act, out_dtype, tm):
    m, d = h.shape
    nt, _, tn = w.shape
    n = nt * tn
    return pl.pallas_call(
        functools.partial(_inproj_kernel, act=act),
        grid=(m // tm, nt),
        in_specs=[pl.BlockSpec((tm, d), lambda i, j: (i, 0)),
                  pl.BlockSpec((None, d, tn), lambda i, j: (j, 0, 0))],
        out_specs=pl.BlockSpec((tm, tn), lambda i, j: (i, j)),
        out_shape=jax.ShapeDtypeStruct((m, n), out_dtype),
        compiler_params=_cparams(("arbitrary", "arbitrary")),
        name="inproj_" + act,
    )(h, w)


def _head_sum(x, lane_lo):
    s0 = jnp.sum(jnp.where(lane_lo, x, 0.0), axis=-1, keepdims=True)
    s1 = jnp.sum(jnp.where(lane_lo, 0.0, x), axis=-1, keepdims=True)
    return jnp.where(lane_lo, s0, s1)


def _rwkv_kernel(r_ref, k_ref, v_ref, wd_ref, ad_ref, gd_ref, pv_ref, mul_ref,
                 wdu_ref, wau_ref, wgu_ref, o_ref, s_ref, prev_ref, prevl_ref):
    C, T, N = RWKV_CHUNK, RWKV_STEP, RWKV_HEAD_DIM
    nbatch = r_ref.shape[0]
    width = r_ref.shape[2]
    npair = width // LANES
    nchunk = T // C

    @pl.when(pl.program_id(1) == 0)
    def _():
        s_ref[...] = jnp.zeros_like(s_ref)
        prev_ref[...] = jnp.zeros_like(prev_ref)
        prevl_ref[...] = jnp.zeros_like(prevl_ref)

    row_t = lax.broadcasted_iota(jnp.int32, (T, 1), 0)

    def shift(z, prev_row, mu):
        zp = jnp.where(row_t == 0, prev_row, pltpu.roll(z, 1, 0))
        return z + mu * (zp - z)

    pv = pv_ref[...]
    w0, a0, k_k, k_a, r_k, gn_w, gn_b = (pv[i:i + 1] for i in range(7))
    mu_r, mu_k, mu_v = pv[7:8], pv[8:9], pv[9:10]
    mul = mul_ref[...]
    lane_lo = lax.broadcasted_iota(jnp.int32, (1, LANES), 1) < N

    def head_sum(x):
        return jnp.concatenate([_head_sum(x[:, q * LANES:(q + 1) * LANES], lane_lo)
                                for q in range(npair)], axis=1)

    row = lax.broadcasted_iota(jnp.int32, (2 * C, 2 * C), 0)
    col = lax.broadcasted_iota(jnp.int32, (2 * C, 2 * C), 1)
    same_head = (row // C) == (col // C)
    strict = same_head & (row > col)
    incl = same_head & (row >= col)
    eye = row == col
    pair = (row // 2) == (col // 2)
    levels = []
    b = 2
    while b < C:
        levels.append(((row // (2 * b)) == (col // (2 * b))) & ((row // b) != (col // b)))
        b *= 2
    tri = (lax.broadcasted_iota(jnp.int32, (C, C), 0)
           >= lax.broadcasted_iota(jnp.int32, (C, C), 1)).astype(F32)

    zero_b = jnp.zeros((), BF16)

    def stack(x):
        xb = x.astype(BF16)
        return jnp.concatenate([jnp.where(lane_lo, xb, zero_b), jnp.where(lane_lo, zero_b, xb)],
                               axis=0)

    def prep_rows(rows_):
        units = []
        for bi in rows_:
            zr, zk, zv = r_ref[bi], k_ref[bi], v_ref[bi]
            zwd, zad, zgd = wd_ref[bi], ad_ref[bi], gd_ref[bi]
            pr = prev_ref[bi]
            pl_ = prevl_ref[bi]
            r = shift(zr, pr[0:1], mu_r)
            k = shift(zk, pr[1:2], mu_k)
            v = shift(zv, pr[2:3], mu_v)
            wd = shift(zwd, pl_[0:1, 0:LORA_PAD], mul[:, 0:LORA_PAD])
            ad = shift(zad, pl_[0:1, LORA_PAD:2 * LORA_PAD], mul[:, LORA_PAD:2 * LORA_PAD])
            gd = shift(zgd, pl_[0:1, 2 * LORA_PAD:], mul[:, 2 * LORA_PAD:])
            prev_ref[bi, 0:1, :] = zr[T - 1:T]
            prev_ref[bi, 1:2, :] = zk[T - 1:T]
            prev_ref[bi, 2:3, :] = zv[T - 1:T]
            prevl_ref[bi, 0:1, 0:LORA_PAD] = zwd[T - 1:T]
            prevl_ref[bi, 0:1, LORA_PAD:2 * LORA_PAD] = zad[T - 1:T]
            prevl_ref[bi, 0:1, 2 * LORA_PAD:] = zgd[T - 1:T]

            w_log = -jax.nn.softplus(-(w0 + _dot(jnp.tanh(wd), wdu_ref[...]))) - 0.5
            lw = -jnp.exp(w_log)
            a = _sigmoid(a0 + _dot(ad, wau_ref[...]))
            g = _dot(_sigmoid(gd), wgu_ref[...])
            kk = k * k_k
            kk = kk / jnp.maximum(jnp.sqrt(head_sum(kk * kk)), 1e-12)
            kmod = k * (1.0 + (a - 1.0) * k_a)
            beta = kk * a
            bonus = head_sum(r * kmod * r_k) * v

            cums = []
            for c in range(nchunk):
                parts = _dot(tri, jnp.concatenate(_split3(lw[c * C:(c + 1) * C]), axis=1))
                cums.append(parts[:, :width] + parts[:, width:2 * width] + parts[:, 2 * width:])

            for q in range(npair):
                ln = slice(q * LANES, (q + 1) * LANES)
                for c in range(nchunk):
                    sl = slice(c * C, (c + 1) * C)
                    units.append(dict(bi=bi, q=q, c=c, sl=sl, ln=ln, lw=lw[sl, ln],
                                      cum=cums[c][:, ln], kk=kk[sl, ln], r=r[sl, ln],
                                      beta=beta[sl, ln], kmod=kmod[sl, ln], v=v[sl, ln],
                                      bonus=bonus[sl, ln], g=g[sl, ln], gn_w=gn_w[:, ln],
                                      gn_b=gn_b[:, ln]))
        return units

    def st_operands(units):
        for un in units:
            cum, lw_c = un["cum"], un["lw"]
            cum_end = cum[C - 1:C]
            e_inv = jnp.exp(-cum)
            e_rem = jnp.exp(cum_end - cum)
            un["a_t"] = stack(-un["kk"] * jnp.exp(cum - lw_c))
            un["r_t"] = stack(un["r"] * jnp.exp(cum))
            un["b_h"] = (un["beta"] * e_inv).astype(BF16)
            un["k_h"] = (un["kmod"] * e_inv).astype(BF16)
            un["b_t"] = stack(un["beta"] * e_rem)
            un["k_t"] = stack(un["kmod"] * e_rem)
            un["v_s"] = stack(un["v"])
            un["g_col"] = jnp.sum(jnp.where(eye, jnp.exp(cum_end), 0.0), axis=1, keepdims=True)
        for un in units:
            un["scores"] = _dot_nt(jnp.concatenate([un["a_t"], un["r_t"]], axis=0),
                                   jnp.concatenate([un["b_h"], un["b_h"], un["k_h"], un["k_h"]],
                                                   axis=0))

    def st_couplings(units):
        for un in units:
            sc = un.pop("scores")
            a_ab = jnp.where(strict, sc[:2 * C, :2 * C], 0.0)
            un["a_ab"] = a_ab.astype(BF16)
            un["a_ak"] = jnp.where(strict, sc[:2 * C, 2 * C:], 0.0).astype(BF16)
            un["a_rb"] = jnp.where(incl, sc[2 * C:, :2 * C], 0.0).astype(BF16)
            un["a_rk"] = jnp.where(incl, sc[2 * C:, 2 * C:], 0.0).astype(BF16)
            un["inv"] = jnp.where(eye, 1.0, jnp.where(pair, a_ab, 0.0))
        for un in units:
            un["av"] = _dot(jnp.concatenate([un["a_ak"], un["a_rk"]], axis=0), un["v_s"])
        for un in units:
            un["sv"] = _dot_tn(un["k_t"], un["v_s"])

    def st_level(units, lvl):
        for un in units:
            un["inv_b"] = un["inv"].astype(BF16)
            un["tmp"] = _dot(jnp.where(lvl, un["a_ab"], zero_b), un["inv_b"])
        for un in units:
            un["inv"] = un["inv"] + _dot(un.pop("inv_b"), un.pop("tmp"))

    def st_solve(units):
        for un in units:
            av = un.pop("av")
            wu = _dot(un["inv"], jnp.concatenate([un["a_t"], av[:2 * C].astype(BF16)], axis=1))
            un["w"], un["u0"], un["y0"] = wu[:, :LANES].astype(BF16), wu[:, LANES:], av[2 * C:]

    def phase_b(units):
        chains = sorted({(un["bi"], un["q"]) for un in units})
        states = [s_ref[bi * npair + q] for bi, q in chains]
        for c in range(nchunk):
            cur = [un for un in units if un["c"] == c]
            us = [un["u0"] + _dot(un["w"], st) for un, st in zip(cur, states)]
            ys = [un["y0"] + _dot(jnp.concatenate([un["r_t"], un["a_rb"]], axis=1),
                                  jnp.concatenate([st.astype(BF16), u.astype(BF16)], axis=0))
                  for un, st, u in zip(cur, states, us)]
            states = [st * un["g_col"] + un["sv"] + _dot_tn(un["b_t"], u)
                      for un, st, u in zip(cur, states, us)]
            for un, y_s in zip(cur, ys):
                y = y_s[:C] + y_s[C:]
                mean = _head_sum(y, lane_lo) * (1.0 / N)
                yc = y - mean
                var = _head_sum(yc * yc, lane_lo) * (1.0 / N)
                yn = yc * lax.rsqrt(var + GN_EPS) * un["gn_w"] + un["gn_b"]
                o_ref[un["bi"], un["sl"], un["ln"]] = (
                    (yn + un["bonus"]) * un["g"]).astype(o_ref.dtype)
        for (bi, q), st in zip(chains, states):
            s_ref[bi * npair + q] = st

    units = prep_rows(range(nbatch))
    st_operands(units)
    st_couplings(units)
    for lvl in levels:
        st_level(units, lvl)
    st_solve(units)
    phase_b(units)


def _rwkv(z_r, pvec, mu_lora, wdu, wau, wgu):
    batch, seq, _ = z_r.shape
    width = wdu.shape[1]
    T = RWKV_STEP
    npair = next(n for n in (RWKV_PAIRS, 4, 2, 1) if (width // LANES) % n == 0)
    lw = npair * LANES
    ng = width // lw
    gate_w = wgu.shape[0]
    lora_w = 2 * LORA_PAD + gate_w
    lora0 = 3 * width // LORA_PAD
    return pl.pallas_call(
        _rwkv_kernel,
        grid=(ng, seq // T),
        in_specs=[
            pl.BlockSpec((batch, T, lw), lambda h, t: (0, t, h)),
            pl.BlockSpec((batch, T, lw), lambda h, t: (0, t, ng + h)),
            pl.BlockSpec((batch, T, lw), lambda h, t: (0, t, 2 * ng + h)),
            pl.BlockSpec((batch, T, LORA_PAD), lambda h, t: (0, t, lora0)),
            pl.BlockSpec((batch, T, LORA_PAD), lambda h, t: (0, t, lora0 + 1)),
            pl.BlockSpec((batch, T, gate_w), lambda h, t: (0, t, (lora0 + 2) * LORA_PAD // gate_w)),
            pl.BlockSpec((RWKV_PARAM_ROWS, lw), lambda h, t: (0, h)),
            pl.BlockSpec((1, lora_w), lambda h, t: (0, 0)),
            pl.BlockSpec((LORA_PAD, lw), lambda h, t: (0, h)),
            pl.BlockSpec((LORA_PAD, lw), lambda h, t: (0, h)),
            pl.BlockSpec((gate_w, lw), lambda h, t: (0, h)),
        ],
        out_specs=pl.BlockSpec((batch, T, lw), lambda h, t: (0, t, h)),
        out_shape=jax.ShapeDtypeStruct((batch, seq, width), BF16),
        scratch_shapes=[pltpu.VMEM((batch * npair, LANES, LANES), F32),
                        pltpu.VMEM((batch, SUBLANES, lw), F32),
                        pltpu.VMEM((batch, SUBLANES, lora_w), F32)],
        compiler_params=_cparams(("arbitrary", "arbitrary")),
        name="rwkv7_mix",
    )(z_r, z_r, z_r, z_r, z_r, z_r, pvec, mu_lora, wdu, wau, wgu)


def _moba_kernel(jj_ref, tt_ref, q_ref, k_ref, v_ref, slope_ref, o_ref, ka_ref, vt_ref, qa_ref,
                 kmean_ref, ma_ref, mb_ref, acca_ref, accb_ref, s_ref, *, nb, tile, ptile, ntrips):
    BLK, DH = MOBA_BLOCK, MOBA_HEAD_DIM
    nbp = kmean_ref.shape[0]
    neg = NEG_INF * LOG2E

    def rows(n):
        return pl.ds(pl.multiple_of(n * BLK, BLK), BLK)

    kmean_ref[...] = jnp.zeros_like(kmean_ref)
    lane = lax.broadcasted_iota(jnp.int32, (BLK, DH), 1)
    key_off = lax.broadcasted_iota(jnp.int32, (BLK, DH), 0).astype(F32)

    def prep(it, carry):
        blocks = [it * ptile + r for r in range(ptile)]
        kfs = [k_ref[rows(n), :] for n in blocks]
        vts = [v_ref[rows(n), :].T for n in blocks]
        for n, kf, vt in zip(blocks, kfs, vts):
            ext = jnp.where(lane < nbp, jnp.where(lane == n, 1.0, 0.0),
                            jnp.where(lane < nbp + SPLIT, lax.convert_element_type(n, F32),
                                      jnp.where(lane < nbp + 2 * SPLIT, key_off, 0.0)))
            ka_ref[n] = jnp.concatenate([kf, ext], axis=1).astype(BF16)
            vt_ref[n] = jnp.concatenate([vt, jnp.ones((MOBA_ONES, BLK), F32)],
                                        axis=0).astype(BF16)
            kmean_ref[pl.ds(n, 1), :] = jnp.mean(kf, axis=0, keepdims=True)
        return carry

    lax.fori_loop(0, nb // ptile, prep, 0)

    slope2 = slope_ref[...] * LOG2E
    coef = _split3(slope2 * float(BLK)) + _split3(slope2)
    erow = lax.broadcasted_iota(jnp.int32, (DH - nbp, BLK), 0)
    ext_q = jnp.zeros((DH - nbp, BLK), F32)
    for idx, cf in enumerate(coef):
        ext_q = jnp.where(erow == idx, cf, ext_q)
    blk_id = lax.broadcasted_iota(jnp.int32, (nbp, BLK), 0)
    k1, k2, k3 = _split3(kmean_ref[...])
    kmean6 = jnp.concatenate([k1, k1, k2, k1, k2, k3], axis=1)
    causal = (lax.broadcasted_iota(jnp.int32, (BLK, BLK), 1)
              >= lax.broadcasted_iota(jnp.int32, (BLK, BLK), 0))
    lane2 = lax.broadcasted_iota(jnp.int32, (BLK, 2 * DH), 1)
    not_onehot = (lane2 < DH) | (lane2 >= DH + nbp)

    def select(gate, i):
        past = blk_id < i
        gm = jnp.where(past, gate, NEG_INF)
        chosen = jnp.zeros((nbp, BLK), dtype=jnp.bool_)
        for _ in range(MOBA_TOPK):
            mx = jnp.max(gm, axis=0, keepdims=True)
            first = jnp.min(jnp.where(gm == mx, blk_id, nbp), axis=0, keepdims=True)
            pick = blk_id == first
            chosen = chosen | pick
            gm = jnp.where(pick, -jnp.inf, gm)
        return jnp.where(chosen & past, 0.0, neg)

    qtile = 2 * tile if nb % (2 * tile) == 0 else tile

    def qprep(it, carry):
        blocks = [it * qtile + r for r in range(qtile)]
        q_ts = [q_ref[rows(i), :].T for i in blocks]
        gates = []
        for q_t in q_ts:
            q1, q2, q3 = _split3(q_t)
            gates.append(_dot(kmean6, jnp.concatenate([q1, q2, q1, q3, q2, q1], axis=0)))
        selbs = [select(g, i) for g, i in zip(gates, blocks)]
        qas = [jnp.concatenate([q_t * (DH ** -0.5 * LOG2E), sb, ext_q], axis=0).astype(BF16)
               for q_t, sb in zip(q_ts, selbs)]
        for i, qa in zip(blocks, qas):
            qa_ref[i] = qa
        keys = [[i - r % tile + jr for jr in range(r % tile)] for r, i in enumerate(blocks)]
        own = [jnp.where(causal, _dot(jnp.where(not_onehot, ka_ref[i], 0), qa), neg)
               for i, qa in zip(blocks, qas)]
        past = [[_dot(ka_ref[j], qa) for j in js] for js, qa in zip(keys, qas)]
        ms = []
        for s, sp in zip(own, past):
            m = jnp.max(s, axis=0, keepdims=True)
            for t in sp:
                m = jnp.maximum(m, jnp.max(t, axis=0, keepdims=True))
            ms.append(m)
        for i, js, m, s, sp in zip(blocks, keys, ms, own, past):
            acc = _dot(vt_ref[i], jnp.exp2(s - m).astype(BF16))
            for j, t in zip(js, sp):
                acc = acc + _dot(vt_ref[j], jnp.exp2(t - m).astype(BF16))
            ma_ref[i] = m
            acca_ref[i] = acc
            mb_ref[i] = jnp.full((1, BLK), -jnp.inf, F32)
            accb_ref[i] = jnp.zeros(accb_ref.shape[1:], F32)
        return carry

    lax.fori_loop(0, nb // qtile, qprep, 0)

    def logits_to(slot, tr):
        kaj = ka_ref[jj_ref[tr]]
        t0 = tt_ref[tr] * tile
        mxs = []
        for r in range(tile):
            s = _dot(kaj, qa_ref[t0 + r])
            s_ref[slot, r] = s
            mxs.append(jnp.max(s, axis=0, keepdims=True))
        return tuple(mxs)

    def half_trip(slot, tr, mxs, tr_next):
        m_ref, acc_ref = (ma_ref, acca_ref) if slot == 0 else (mb_ref, accb_ref)
        mxs_next = logits_to(1 - slot, tr_next)
        vtj = vt_ref[jj_ref[tr]]
        t0 = tt_ref[tr] * tile
        for r in range(tile):
            i = t0 + r
            m = m_ref[i]
            m_new = jnp.maximum(m, mxs[r])
            p = jnp.exp2(s_ref[slot, r] - m_new)
            m_ref[i] = m_new
            acc_ref[i] = acc_ref[i] * jnp.exp2(m - m_new) + _dot(vtj, p.astype(BF16))
        return mxs_next

    def body(pi, mxs):
        tr = MOBA_UNROLL * pi
        for u in range(MOBA_UNROLL - 1):
            mxs = half_trip(u % 2, tr + u, mxs, tr + u + 1)
        last = tr + MOBA_UNROLL - 1
        return half_trip((MOBA_UNROLL - 1) % 2, last, mxs, jnp.minimum(last + 1, ntrips - 1))

    if ntrips:
        lax.fori_loop(0, ntrips // MOBA_UNROLL, body, logits_to(0, 0))

    def finish(it, carry):
        blocks = [it * ptile + r for r in range(ptile)]
        outs = []
        for i in blocks:
            ma, mb = ma_ref[i], mb_ref[i]
            m = jnp.maximum(ma, mb)
            acc = acca_ref[i] * jnp.exp2(ma - m) + accb_ref[i] * jnp.exp2(mb - m)
            outs.append((acc[:DH] / acc[DH:DH + 1]).T)
        for i, o in zip(blocks, outs):
            o_ref[rows(i), :] = o.astype(o_ref.dtype)
        return carry

    lax.fori_loop(0, nb // ptile, finish, 0)


def _moba_schedule(nb, tile):
    trips = [(j, t) for j in range(nb) for t in range(j // tile + 1, nb // tile)]
    while len(trips) % MOBA_UNROLL:
        trips.append((nb - 1, nb // tile - 1))
    return trips


def _moba(z_m, slopes, *, batch, seq):
    m, cols = z_m.shape
    heads = cols // (3 * MOBA_HEAD_DIM)
    BLK, DH = MOBA_BLOCK, MOBA_HEAD_DIM
    nb = seq // BLK
    nbp = -(-nb // SUBLANES) * SUBLANES
    assert nbp + 2 * SPLIT <= DH
    tile = next(g for g in (MOBA_GROUP, 2, 1) if nb % g == 0)
    ptile = next(g for g in (MOBA_PREP, 4, 2, 1) if nb % g == 0)
    trips = _moba_schedule(nb, tile)
    sched = jnp.asarray(trips or [(0, 0)], jnp.int32)
    grid_spec = pltpu.PrefetchScalarGridSpec(
        num_scalar_prefetch=2,
        grid=(batch, heads),
        in_specs=[
            pl.BlockSpec((seq, DH), lambda b, h, jj, tt: (b, h)),
            pl.BlockSpec((seq, DH), lambda b, h, jj, tt: (b, heads + h)),
            pl.BlockSpec((seq, DH), lambda b, h, jj, tt: (b, 2 * heads + h)),
            pl.BlockSpec((None, 1, BLK), lambda b, h, jj, tt: (h, 0, 0)),
        ],
        out_specs=pl.BlockSpec((seq, DH), lambda b, h, jj, tt: (b, h)),
        scratch_shapes=[pltpu.VMEM((nb, BLK, 2 * DH), BF16),
                        pltpu.VMEM((nb, DH + MOBA_ONES, BLK), BF16),
                        pltpu.VMEM((nb, 2 * DH, BLK), BF16),
                        pltpu.VMEM((nbp, DH), F32),
                        pltpu.VMEM((nb, 1, BLK), F32),
                        pltpu.VMEM((nb, 1, BLK), F32),
                        pltpu.VMEM((nb, DH + MOBA_ONES, BLK), F32),
                        pltpu.VMEM((nb, DH + MOBA_ONES, BLK), F32),
                        pltpu.VMEM((2, tile, BLK, BLK), F32)])
    return pl.pallas_call(
        functools.partial(_moba_kernel, nb=nb, tile=tile, ptile=ptile, ntrips=len(trips)),
        grid_spec=grid_spec,
        out_shape=jax.ShapeDtypeStruct((m, heads * DH), BF16),
        compiler_params=_cparams(("arbitrary", "arbitrary")),
        name="moba_attn",
    )(sched[:, 0], sched[:, 1], z_m, z_m, z_m, slopes)


def _resident(shape):
    return pl.BlockSpec(shape, lambda i: (0,) * len(shape), pipeline_mode=pl.Buffered(1))


def _merge_out_kernel(x_ref, yr_ref, ym_ref, g_ref, wr_ref, wm_ref, wo_ref, o_ref):
    d = o_ref.shape[1]
    br = _dot(yr_ref[...], wr_ref[...])
    bm = _dot(ym_ref[...], wm_ref[...])
    merged = (_sigmoid(g_ref[:, :d].astype(F32)) * br
              + _sigmoid(g_ref[:, d:].astype(F32)) * bm)
    o_ref[...] = x_ref[...] + _dot(merged, wo_ref[...])


def _merge_out(x2, y_r, y_m, gates, w_r, w_m, w_o, *, tm):
    m, d = x2.shape
    kr, km = y_r.shape[1], y_m.shape[1]
    return pl.pallas_call(
        _merge_out_kernel,
        grid=(m // tm,),
        in_specs=[pl.BlockSpec((tm, d), lambda i: (i, 0)),
                  pl.BlockSpec((tm, kr), lambda i: (i, 0)),
                  pl.BlockSpec((tm, km), lambda i: (i, 0)),
                  pl.BlockSpec((tm, 2 * d), lambda i: (i, 0)),
                  _resident((kr, d)), _resident((km, d)), _resident((d, d))],
        out_specs=pl.BlockSpec((tm, d), lambda i: (i, 0)),
        out_shape=jax.ShapeDtypeStruct((m, d), F32),
        compiler_params=_cparams(("arbitrary",)),
        name="merge_out_proj",
    )(x2, y_r, y_m, gates, w_r, w_m, w_o)


def _mlp_kernel(x_ref, g_ref, w1_ref, w2_ref, o_ref, h_ref):
    @pl.when(pl.program_id(1) == 0)
    def _():
        xf = x_ref[...]
        h_ref[...] = _rms(xf, g_ref[...]).astype(BF16)
        o_ref[...] = xf

    u = jnp.maximum(_dot(h_ref[...], w1_ref[...]), 0.0)
    o_ref[...] += _dot((u * u).astype(BF16), w2_ref[...])


def _mlp(x2, g, w1, w2, *, tm):
    m, d = x2.shape
    nf, _, tf = w1.shape
    return pl.pallas_call(
        _mlp_kernel,
        grid=(m // tm, nf),
        in_specs=[pl.BlockSpec((tm, d), lambda i, j: (i, 0)),
                  pl.BlockSpec((1, d), lambda i, j: (0, 0)),
                  pl.BlockSpec((None, d, tf), lambda i, j: (j, 0, 0)),
                  pl.BlockSpec((tf, d), lambda i, j: (j, 0))],
        out_specs=pl.BlockSpec((tm, d), lambda i, j: (i, 0)),
        out_shape=jax.ShapeDtypeStruct((m, d), F32),
        scratch_shapes=[pltpu.VMEM((tm, d), BF16)],
        compiler_params=_cparams(("arbitrary", "arbitrary")),
        name="mlp_relu2",
    )(x2, g, w1, w2)


def _ple_kernel(x_ref, p_ref, g_ref, gf_ref, wg_ref, wu_ref, o_ref, *, final_norm):
    xf = x_ref[...]
    hp = _rms(xf, g_ref[...]).astype(BF16)
    gate = _sigmoid(_dot(hp, wg_ref[...]))
    up = _dot(p_ref[...].astype(BF16), wu_ref[...])
    y = xf + gate * up
    o_ref[...] = _rms(y, gf_ref[...]) if final_norm else y


def _ple(x2, p2, g, gf, wg, wu, *, tm, final_norm):
    m, d = x2.shape
    pd = p2.shape[1]
    return pl.pallas_call(
        functools.partial(_ple_kernel, final_norm=final_norm),
        grid=(m // tm,),
        in_specs=[pl.BlockSpec((tm, d), lambda i: (i, 0)),
                  pl.BlockSpec((tm, pd), lambda i: (i, 0)),
                  pl.BlockSpec((1, d), lambda i: (0, 0)),
                  pl.BlockSpec((1, d), lambda i: (0, 0)),
                  _resident((d, d)), _resident((pd, d))],
        out_specs=pl.BlockSpec((tm, d), lambda i: (i, 0)),
        out_shape=jax.ShapeDtypeStruct((m, d), F32),
        compiler_params=_cparams(("arbitrary",)),
        name="ple_final",
    )(x2, p2, g, gf, wg, wu)


def _tile_plan(m):
    def rows(cap):
        return next(t for t in (2048, 1024, 512, 256, 128, 64, 32, 16, 8)
                    if t <= cap and m % t == 0)

    return dict(tn=512, tm_norm=rows(512), tm_in=rows(2048), tm_out=rows(512), tm_mlp=rows(1024),
                tm_ple=rows(512))


def _wide_tile(n, tn):
    return 2 * tn if n % (2 * tn) == 0 else tn


def _pad_to(a, n, axis):
    pad = [(0, 0)] * a.ndim
    pad[axis] = (0, n - a.shape[axis])
    return jnp.pad(a, pad)


def kernel(x, p, g_mix, w_in, mu_shift, w0, w_decay_up, a0, w_aaa_up, w_gate_up, k_k, k_a, r_k,
           gn_w, gn_b, w_branch_rwkv, w_branch_moba, w_out, g_mlp, w_mlp_in, w_mlp_out, g_ple,
           w_ple_gate, w_ple_up, g_final):
    B, S, D = x.shape
    depth = w_in.shape[0]
    W = w0.shape[1]
    Wm = w_branch_moba.shape[1]
    dl, al, gl = w_decay_up.shape[1], w_aaa_up.shape[1], w_gate_up.shape[1]
    assert S % MOBA_BLOCK == 0 and S % RWKV_STEP == 0
    assert W % LANES == 0 and dl <= LORA_PAD and al <= LORA_PAD and gl % LANES == 0
    M = B * S
    tiles = _tile_plan(M)
    x2 = x.reshape(M, D)
    moba_heads = Wm // MOBA_HEAD_DIM
    slopes = 2.0 ** (-8.0 * (jnp.arange(moba_heads, dtype=F32) + 1.0) / moba_heads)
    slopes = jnp.broadcast_to(slopes[:, None, None], (moba_heads, 1, MOBA_BLOCK))

    for i in range(depth):
        wi = w_in[i]
        c0 = 3 * W
        c1, c2, c3 = c0 + dl, c0 + dl + al, c0 + dl + al + gl
        tn = tiles["tn"]
        w_r = _col_tiles(jnp.concatenate([wi[:, :c0], _pad_to(wi[:, c0:c1], LORA_PAD, 1),
                                          _pad_to(wi[:, c1:c2], LORA_PAD, 1), wi[:, c2:c3]],
                                         axis=1), tn)
        w_m = _col_tiles(wi[:, c3:c3 + 3 * Wm], _wide_tile(3 * Wm, tn))
        w_g = _col_tiles(wi[:, c3 + 3 * Wm:], _wide_tile(wi.shape[1] - c3 - 3 * Wm, tn))
        mu = mu_shift[i]
        mu_lora = jnp.concatenate([_pad_to(mu[c0:c1], LORA_PAD, 0), _pad_to(mu[c1:c2], LORA_PAD, 0),
                                   mu[c2:c3]])[None, :]
        pvec = jnp.stack([w0[i], a0[i], k_k[i], k_a[i], r_k[i].reshape(-1), gn_w[i], gn_b[i],
                          mu[:W], mu[W:2 * W], mu[2 * W:3 * W]])
        pvec = _pad_to(pvec, RWKV_PARAM_ROWS, 0)
        wdu = _pad_to(w_decay_up[i], LORA_PAD, 0)
        wau = _pad_to(w_aaa_up[i], LORA_PAD, 0)
        gmix = g_mix[i][None, :]

        z_r, h = _norm_inproj(x2, gmix, w_r, out_dtype=F32, tm=tiles["tm_mlp"])
        z_m = _inproj(h, w_m, act="none", out_dtype=F32, tm=tiles["tm_in"])
        gates = _inproj(h, w_g, act="none", out_dtype=BF16, tm=tiles["tm_in"])

        y_r = _rwkv(z_r.reshape(B, S, -1), pvec, mu_lora, wdu, wau, w_gate_up[i]).reshape(M, W)
        y_m = _moba(z_m, slopes, batch=B, seq=S)

        x2 = _merge_out(x2, y_r, y_m, gates, w_branch_rwkv[i].astype(BF16),
                        w_branch_moba[i].astype(BF16), w_out[i].astype(BF16), tm=tiles["tm_out"])
        x2 = _mlp(x2, g_mlp[i][None, :], _col_tiles(w_mlp_in[i], tn), w_mlp_out[i].astype(BF16),
                  tm=tiles["tm_mlp"])
        x2 = _ple(x2, p[i].reshape(M, -1), g_ple[i][None, :], g_final[None, :],
                  w_ple_gate[i].astype(BF16), w_ple_up[i].astype(BF16), tm=tiles["tm_ple"],
                  final_norm=(i == depth - 1))
    return x2.reshape(B, S, D)
```
